```python
import jax
import jax.numpy as jnp
from jax import lax
import numpy as np

D_MODEL = 2048
BATCH = 8
SEQ = 2048
DEPTH = 1

HEAD_DIM = 128
NSA_WIDTH = D_MODEL // 2
NSA_HEADS = NSA_WIDTH // HEAD_DIM
NSA_KV_HEADS = max(1, NSA_HEADS // 4)
NSA_GROUP = NSA_HEADS // NSA_KV_HEADS
CMP_LEN = 32
CMP_STRIDE = 16
SLC_LEN = 64
N_SELECT = 16
FORCED_SCORE = 1e4
WINDOW = 512
WIN_BLOCK = 128
SLC_Q_CHUNK = 32
RNN_WIDTH = D_MODEL - NSA_WIDTH
RNN_BLOCKS = RNN_WIDTH // HEAD_DIM
RNN_BLOCK_DIM = RNN_WIDTH // RNN_BLOCKS
CONV_WIDTH = 4
LRU_C = 8.0
PEER_HEADS = 8
PEER_KEYS = 128
PEER_EXPERTS = PEER_KEYS * PEER_KEYS
PEER_TOPK = 16
PEER_DQ = 256
PEER_TOK_CHUNK = 128
ROPE_THETA = 10000.0
EPS = 1e-6

Q_COLS = NSA_HEADS * HEAD_DIM
KV_COLS = NSA_KV_HEADS * HEAD_DIM
GATE_COLS = 3 * NSA_HEADS
N_IN = Q_COLS + 6 * KV_COLS + GATE_COLS + 2 * RNN_WIDTH

kernel_name = 'hybrid_nsa_rglru_peer_adaln'


def rms_norm(x, g):
    x32 = x.astype(jnp.float32)
    y = x32 * lax.rsqrt(jnp.mean(x32 * x32, axis=-1, keepdims=True) + EPS)
    return (y * g.astype(jnp.float32)).astype(x.dtype)


def rope(x, pos):
    half = x.shape[-1] // 2
    freqs = ROPE_THETA ** (-jnp.arange(half, dtype=jnp.float32) / half)
    ang = pos.astype(jnp.float32)[:, None] * freqs[None, :]
    cos = jnp.cos(ang)[:, None, :]
    sin = jnp.sin(ang)[:, None, :]
    x32 = x.astype(jnp.float32)
    x1, x2 = x32[..., :half], x32[..., half:]
    return jnp.concatenate([x1 * cos - x2 * sin, x2 * cos + x1 * sin], axis=-1).astype(x.dtype)


def masked_softmax(s, mask):
    s = jnp.where(mask, s.astype(jnp.float32), -jnp.inf)
    m = jnp.max(s, axis=-1, keepdims=True)
    m = jnp.where(jnp.isfinite(m), m, 0.0)
    p = jnp.exp(s - m)
    d = jnp.sum(p, axis=-1, keepdims=True)
    return p / jnp.where(d > 0, d, 1.0)


def compress_blocks(k, pe, w):
    B, S, G, dh = k.shape
    n_cmp = (S - CMP_LEN) // CMP_STRIDE + 1
    idx = (jnp.arange(n_cmp) * CMP_STRIDE)[:, None] + jnp.arange(CMP_LEN)[None, :]
    blk = k[:, idx] + pe[None, None, :, None, :]
    blk = jnp.transpose(blk, (0, 3, 1, 2, 4)).reshape(B, G, n_cmp, CMP_LEN * dh)
    return blk @ w


def nsa_attention(q, kc, vc, ks, vs, kw, vw, gates):
    B, G, S, R, dh = q.shape
    scale = dh ** -0.5
    t = jnp.arange(S)
    n_cmp = kc.shape[2]
    cmp_start = jnp.arange(n_cmp) * CMP_STRIDE
    mask_c = (cmp_start + CMP_LEN - 1)[None, :] <= t[:, None]
    p_c = masked_softmax(jnp.einsum('bgsrd,bgcd->bgsrc', q, kc) * scale, mask_c[None, None, :, None, :])
    o_cmp = jnp.einsum('bgsrc,bgcd->bgsrd', p_c.astype(vc.dtype), vc)
    n_slc = S // SLC_LEN
    slc_start = jnp.arange(n_slc) * SLC_LEN
    overlap = jnp.maximum(
        jnp.minimum(cmp_start[:, None] + CMP_LEN, slc_start[None, :] + SLC_LEN)
        - jnp.maximum(cmp_start[:, None], slc_start[None, :]), 0).astype(jnp.float32) / CMP_LEN
    imp = jnp.einsum('bgsrc,cj->bgsj', p_c, overlap)
    cur = t // SLC_LEN
    jb = jnp.arange(n_slc)
    forced = (jb[None, :] == 0) | (jb[None, :] == cur[:, None]) | (jb[None, :] == cur[:, None] - 1)
    valid = slc_start[None, :] <= t[:, None]
    imp = jnp.where(forced, FORCED_SCORE, jnp.where(valid, imp, -FORCED_SCORE))
    n_sel = min(N_SELECT, n_slc)
    _, sel = lax.top_k(imp, n_sel)
    kb = ks.reshape(B, G, n_slc, SLC_LEN, dh)
    vb = vs.reshape(B, G, n_slc, SLC_LEN, dh)
    nq = S // SLC_Q_CHUNK
    q_chunks = jnp.moveaxis(q.reshape(B, G, nq, SLC_Q_CHUNK, R, dh), 2, 0)
    sel_chunks = jnp.moveaxis(sel.reshape(B, G, nq, SLC_Q_CHUNK, n_sel), 2, 0)
    t_chunks = t.reshape(nq, SLC_Q_CHUNK)
    bi = jnp.arange(B)[:, None, None, None]
    gi = jnp.arange(G)[None, :, None, None]
    n_key = n_sel * SLC_LEN

    def slc_chunk(args):
        q_i, sel_i, t_i = args
        kg = kb[bi, gi, sel_i].reshape(B, G, SLC_Q_CHUNK, n_key, dh)
        vg = vb[bi, gi, sel_i].reshape(B, G, SLC_Q_CHUNK, n_key, dh)
        kpos = (sel_i[..., None] * SLC_LEN + jnp.arange(SLC_LEN)).reshape(B, G, SLC_Q_CHUNK, n_key)
        mask = (kpos <= t_i[None, None, :, None])[:, :, :, None, :]
        p = masked_softmax(jnp.einsum('bgqrd,bgqkd->bgqrk', q_i, kg) * scale, mask)
        return jnp.einsum('bgqrk,bgqkd->bgqrd', p.astype(vg.dtype), vg)

    o_slc = lax.map(slc_chunk, (q_chunks, sel_chunks, t_chunks))
    o_slc = jnp.moveaxis(o_slc, 0, 2).reshape(B, G, S, R, dh)
    nb = S // WIN_BLOCK
    npre = WINDOW // WIN_BLOCK
    pad = ((0, 0), (0, 0), (WINDOW, 0), (0, 0))
    kp = jnp.pad(kw, pad).reshape(B, G, nb + npre, WIN_BLOCK, dh)
    vp = jnp.pad(vw, pad).reshape(B, G, nb + npre, WIN_BLOCK, dh)
    kband = jnp.concatenate([kp[:, :, i:i + nb] for i in range(npre + 1)], axis=3)
    vband = jnp.concatenate([vp[:, :, i:i + nb] for i in range(npre + 1)], axis=3)
    qw = q.reshape(B, G, nb, WIN_BLOCK, R, dh)
    qi = jnp.arange(WIN_BLOCK)
    kj = jnp.arange((npre + 1) * WIN_BLOCK)
    rel = kj[None, :] - WINDOW - qi[:, None]
    kabs = (jnp.arange(nb) * WIN_BLOCK)[:, None] + kj[None, :] - WINDOW
    mask_w = ((rel <= 0) & (rel > -WINDOW))[None, :, :] & (kabs >= 0)[:, None, :]
    p_w = masked_softmax(jnp.einsum('bgnqrd,bgnkd->bgnqrk', qw, kband) * scale,
                         mask_w[None, None, :, :, None, :])
    o_win = jnp.einsum('bgnqrk,bgnkd->bgnqrd', p_w.astype(vband.dtype), vband).reshape(B, G, S, R, dh)
    return gates[..., 0:1] * o_cmp + gates[..., 1:2] * o_slc + gates[..., 2:3] * o_win


def rglru_branch(xr, xg, conv_w, conv_b, wa, ba, wi, bi, lam):
    B, S, W = xr.shape
    xp = jnp.pad(xr, ((0, 0), (CONV_WIDTH - 1, 0), (0, 0)))
    u = conv_b + sum(xp[:, i:i + S] * conv_w[i] for i in range(CONV_WIDTH))
    ub = u.reshape(B, S, RNN_BLOCKS, RNN_BLOCK_DIM)
    r = jax.nn.sigmoid((jnp.einsum('bsnd,nde->bsne', ub, wa).reshape(B, S, W) + ba).astype(jnp.float32))
    ig = jax.nn.sigmoid((jnp.einsum('bsnd,nde->bsne', ub, wi).reshape(B, S, W) + bi).astype(jnp.float32))
    log_a = -LRU_C * r * jax.nn.softplus(-lam.astype(jnp.float32))
    a = jnp.exp(log_a)
    b = jnp.sqrt(-jnp.expm1(2.0 * log_a)) * ig * u.astype(jnp.float32)

    def combine(left, right):
        a1, b1 = left
        a2, b2 = right
        return a1 * a2, a2 * b1 + b2

    _, hs = lax.associative_scan(combine, (a, b), axis=1)
    return (jax.nn.gelu(xg.astype(jnp.float32), approximate=False) * hs).astype(xr.dtype)


def hybrid_mixer(h, w_in, w_out, q_norm_g, k_norm_g, cmp_pe_k, cmp_pe_v, cmp_w_k, cmp_w_v, gate_b,
                 conv_w, conv_b, lru_wa, lru_ba, lru_wi, lru_bi, lru_lam, out_g_attn, out_g_rnn):
    B, S, _ = h.shape
    G, R, dh = NSA_KV_HEADS, NSA_GROUP, HEAD_DIM
    z = h @ w_in
    cuts = [int(v) for v in np.cumsum([Q_COLS] + [KV_COLS] * 6 + [GATE_COLS, RNN_WIDTH])]
    q, kc, vc, ks, vs, kw, vw, gl, xr, xg = jnp.split(z, cuts, axis=-1)
    pos = jnp.arange(S)
    q = rope(rms_norm(q.reshape(B, S, NSA_HEADS, dh), q_norm_g), pos)
    kc = rms_norm(compress_blocks(rope(kc.reshape(B, S, G, dh), pos), cmp_pe_k, cmp_w_k), k_norm_g[0])
    vc = compress_blocks(vc.reshape(B, S, G, dh), cmp_pe_v, cmp_w_v)
    ks = rope(rms_norm(ks.reshape(B, S, G, dh), k_norm_g[1]), pos)
    kw = rope(rms_norm(kw.reshape(B, S, G, dh), k_norm_g[2]), pos)
    qg = q.reshape(B, S, G, R, dh).transpose(0, 2, 1, 3, 4)
    gates = jax.nn.sigmoid(gl + gate_b).reshape(B, S, G, R, 3).transpose(0, 2, 1, 3, 4)
    o_attn = nsa_attention(qg, kc, vc,
                           jnp.swapaxes(ks, 1, 2), jnp.swapaxes(vs.reshape(B, S, G, dh), 1, 2),
                           jnp.swapaxes(kw, 1, 2), jnp.swapaxes(vw.reshape(B, S, G, dh), 1, 2), gates)
    o_attn = o_attn.transpose(0, 2, 1, 3, 4).reshape(B, S, NSA_WIDTH)
    o_rnn = rglru_branch(xr, xg, conv_w, conv_b, lru_wa, lru_ba, lru_wi, lru_bi, lru_lam)
    y = jnp.concatenate([rms_norm(o_attn, out_g_attn), rms_norm(o_rnn, out_g_rnn)], axis=-1)
    return y @ w_out


def peer_ffn(h, wq, keys, down, up):
    B, S, D = h.shape
    T = B * S
    hf = h.reshape(T, D)
    q = (hf @ wq).reshape(T, PEER_HEADS, 2, PEER_DQ // 2)
    s = jnp.einsum('thpd,hpkd->thpk', q, keys).astype(jnp.float32)
    v1, i1 = lax.top_k(s[:, :, 0], PEER_TOPK)
    v2, i2 = lax.top_k(s[:, :, 1], PEER_TOPK)
    cand = (v1[..., :, None] + v2[..., None, :]).reshape(T, PEER_HEADS, PEER_TOPK * PEER_TOPK)
    cidx = (i1[..., :, None] * PEER_KEYS + i2[..., None, :]).reshape(T, PEER_HEADS, PEER_TOPK * PEER_TOPK)
    vals, pick = lax.top_k(cand, PEER_TOPK)
    eidx = jnp.take_along_axis(cidx, pick, axis=-1)
    g = jax.nn.softmax(vals, axis=-1)
    nc = T // PEER_TOK_CHUNK

    def chunk(args):
        x_c, e_c, g_c = args
        act = jax.nn.gelu(jnp.einsum('td,thkd->thk', x_c, down[e_c]).astype(jnp.float32),
                          approximate=False) * g_c
        return jnp.einsum('thk,thkd->td', act.astype(up.dtype), up[e_c])

    out = lax.map(chunk, (hf.reshape(nc, PEER_TOK_CHUNK, D),
                          eidx.reshape(nc, PEER_TOK_CHUNK, PEER_HEADS, PEER_TOPK),
                          g.reshape(nc, PEER_TOK_CHUNK, PEER_HEADS, PEER_TOPK)))
    return out.reshape(B, S, D).astype(h.dtype)


def setup_inputs(seed: int = 0) -> dict:
    key = jax.random.key(seed)
    ks = jax.random.split(key, 32)
    f32 = jnp.float32
    L = DEPTH

    def nrm(k, shape, s):
        return jax.random.normal(k, shape, f32) * s

    u = jax.random.uniform(ks[26], (L, RNN_WIDTH), f32, 0.9, 0.999)
    return {
        'x': nrm(ks[0], (BATCH, SEQ, D_MODEL), 1.0),
        'c': nrm(ks[1], (BATCH, D_MODEL), 1.0),
        'ada_w': nrm(ks[2], (L, D_MODEL, 6 * D_MODEL), 0.5 * D_MODEL ** -0.5),
        'ada_b': nrm(ks[3], (L, 6 * D_MODEL), 0.02),
        'norm_mix_g': 1.0 + nrm(ks[4], (L, D_MODEL), 0.02),
        'norm_ffn_g': 1.0 + nrm(ks[5], (L, D_MODEL), 0.02),
        'w_in': nrm(ks[6], (L, D_MODEL, N_IN), D_MODEL ** -0.5),
        'w_out': nrm(ks[7], (L, NSA_WIDTH + RNN_WIDTH, D_MODEL), (NSA_WIDTH + RNN_WIDTH) ** -0.5),
        'q_norm_g': 1.0 + nrm(ks[8], (L, HEAD_DIM), 0.02),
        'k_norm_g': 1.0 + nrm(ks[9], (L, 3, HEAD_DIM), 0.02),
        'cmp_pe_k': nrm(ks[10], (L, CMP_LEN, HEAD_DIM), 0.02),
        'cmp_pe_v': nrm(ks[11], (L, CMP_LEN, HEAD_DIM), 0.02),
        'cmp_w_k': nrm(ks[12], (L, CMP_LEN * HEAD_DIM, HEAD_DIM), (CMP_LEN * HEAD_DIM) ** -0.5),
        'cmp_w_v': nrm(ks[13], (L, CMP_LEN * HEAD_DIM, HEAD_DIM), (CMP_LEN * HEAD_DIM) ** -0.5),
        'gate_b': nrm(ks[14], (L, GATE_COLS), 0.1),
        'conv_w': nrm(ks[15], (L, CONV_WIDTH, RNN_WIDTH), CONV_WIDTH ** -0.5),
        'conv_b': nrm(ks[16], (L, RNN_WIDTH), 0.02),
        'lru_wa': nrm(ks[17], (L, RNN_BLOCKS, RNN_BLOCK_DIM, RNN_BLOCK_DIM), RNN_BLOCK_DIM ** -0.5),
        'lru_ba': nrm(ks[18], (L, RNN_WIDTH), 0.1),
        'lru_wi': nrm(ks[19], (L, RNN_BLOCKS, RNN_BLOCK_DIM, RNN_BLOCK_DIM), RNN_BLOCK_DIM ** -0.5),
        'lru_bi': nrm(ks[20], (L, RNN_WIDTH), 0.1),
        'lru_lam': jnp.log(u) - jnp.log1p(-u),
        'out_g_attn': 1.0 + nrm(ks[21], (L, NSA_WIDTH), 0.02),
        'out_g_rnn': 1.0 + nrm(ks[22], (L, RNN_WIDTH), 0.02),
        'peer_wq': nrm(ks[23], (L, D_MODEL, PEER_HEADS * PEER_DQ), D_MODEL ** -0.5),
        'peer_keys': nrm(ks[24], (L, PEER_HEADS, 2, PEER_KEYS, PEER_DQ // 2), (PEER_DQ // 2) ** -0.5),
        'peer_down': nrm(ks[25], (L, PEER_EXPERTS, D_MODEL), D_MODEL ** -0.5),
        'peer_up': nrm(ks[27], (L, PEER_EXPERTS, D_MODEL), PEER_HEADS ** -0.5),
    }


def reference(x, c, ada_w, ada_b, norm_mix_g, norm_ffn_g, w_in, w_out, q_norm_g, k_norm_g,
              cmp_pe_k, cmp_pe_v, cmp_w_k, cmp_w_v, gate_b, conv_w, conv_b, lru_wa, lru_ba,
              lru_wi, lru_bi, lru_lam, out_g_attn, out_g_rnn, peer_wq, peer_keys, peer_down, peer_up):
    for l in range(DEPTH):
        mod = jax.nn.silu(c) @ ada_w[l] + ada_b[l]
        sh1, sc1, gt1, sh2, sc2, gt2 = jnp.split(mod, 6, axis=-1)
        h = rms_norm(x, norm_mix_g[l]) * (1 + sc1[:, None]) + sh1[:, None]
        x = x + gt1[:, None] * hybrid_mixer(
            h, w_in[l], w_out[l], q_norm_g[l], k_norm_g[l], cmp_pe_k[l], cmp_pe_v[l], cmp_w_k[l],
            cmp_w_v[l], gate_b[l], conv_w[l], conv_b[l], lru_wa[l], lru_ba[l], lru_wi[l], lru_bi[l],
            lru_lam[l], out_g_attn[l], out_g_rnn[l])
        h = rms_norm(x, norm_ffn_g[l]) * (1 + sc2[:, None]) + sh2[:, None]
        x = x + gt2[:, None] * peer_ffn(h, peer_wq[l], peer_keys[l], peer_down[l], peer_up[l])
    return x
```

```python
import functools

import numpy as np
import jax
import jax.numpy as jnp
from jax import lax
from jax.experimental import pallas as pl
from jax.experimental.pallas import tpu as pltpu

f32 = jnp.float32
bf16 = jnp.bfloat16
i32 = jnp.int32

HEAD_DIM = 128
NSA_HEADS = 8
NSA_KV_HEADS = 2
NSA_GROUP = NSA_HEADS // NSA_KV_HEADS
NSA_WIDTH = NSA_HEADS * HEAD_DIM
KV_COLS = NSA_KV_HEADS * HEAD_DIM
GATE_COLS = 3 * NSA_HEADS
CMP_LEN = 32
CMP_STRIDE = 16
SLC_LEN = 64
N_SELECT = 16
FORCED_SCORE = 1e4
WINDOW = 512
RNN_WIDTH = 1024
RNN_BLOCKS = 8
RNN_BLOCK_DIM = RNN_WIDTH // RNN_BLOCKS
CONV_WIDTH = 4
LRU_C = 8.0
PEER_HEADS = 8
PEER_KEYS = 128
PEER_TOPK = 16
ROPE_THETA = 10000.0
EPS = 1e-6
NEG_BIG = -1e30

LANES = 128
VMEM_LIMIT = 56 * 1024 * 1024

NT_DIMS = (((1,), (1,)), ((), ()))


def _cparams(*sem):
    return pltpu.CompilerParams(dimension_semantics=sem, vmem_limit_bytes=VMEM_LIMIT)


def _rms(x, g):
    return x * lax.rsqrt(jnp.mean(x * x, axis=-1, keepdims=True) + EPS) * g


def _mod_kernel(c_ref, w_ref, b_ref, o_ref):
    c = c_ref[...]
    a = c * jax.nn.sigmoid(c)
    o_ref[...] = jnp.dot(a, w_ref[...], preferred_element_type=f32) + b_ref[...]


def _mod(c, w, b, tn=1024):
    B, D = c.shape
    N = w.shape[1]
    return pl.pallas_call(
        _mod_kernel,
        grid=(N // tn,),
        in_specs=[
            pl.BlockSpec((B, D), lambda j: (0, 0)),
            pl.BlockSpec((D, tn), lambda j: (0, j)),
            pl.BlockSpec((1, tn), lambda j: (0, j)),
        ],
        out_specs=pl.BlockSpec((B, tn), lambda j: (0, j)),
        out_shape=jax.ShapeDtypeStruct((B, N), f32),
        compiler_params=_cparams("arbitrary"),
        name="mod",
    )(c, w, b.reshape(1, N))


def _norm_mm_kernel(x_ref, sc_ref, sh_ref, g_ref, w_ref, *rest, n_extra, emit_h):
    rest = list(rest)
    if n_extra:
        wg_ref, gb_ref = rest.pop(0), rest.pop(0)
    z_ref = rest.pop(0)
    if n_extra:
        gate_ref = rest.pop(0)
    if emit_h:
        h_ref = rest.pop(0)
    hb_ref = rest.pop(0)

    @pl.when(pl.program_id(1) == 0)
    def _():
        h = _rms(x_ref[...], g_ref[...]) * (1.0 + sc_ref[0]) + sh_ref[0]
        hb = h.astype(bf16)
        hb_ref[...] = hb
        if emit_h:
            h_ref[...] = h
        if n_extra:
            gl = jnp.dot(hb, wg_ref[...], preferred_element_type=f32)
            gate_ref[...] = jax.nn.sigmoid(gl + gb_ref[...])

    z_ref[...] = jnp.dot(hb_ref[...], w_ref[...], preferred_element_type=f32)


def _norm_mm(x, sc, sh, g, w, seq, *, tm, tn, wg=None, gb=None, emit_h=False):
    T, D = x.shape
    N = w.shape[1]
    B = sc.shape[0]
    assert seq % tm == 0 and N % tn == 0
    per_b = seq // tm
    n_extra = 0 if wg is None else wg.shape[1]
    in_specs = [
        pl.BlockSpec((tm, D), lambda i, j: (i, 0)),
        pl.BlockSpec((1, 1, D), lambda i, j: (i // per_b, 0, 0)),
        pl.BlockSpec((1, 1, D), lambda i, j: (i // per_b, 0, 0)),
        pl.BlockSpec((1, D), lambda i, j: (0, 0)),
        pl.BlockSpec((D, tn), lambda i, j: (0, j)),
    ]
    args = [x, sc.reshape(B, 1, D), sh.reshape(B, 1, D), g.reshape(1, D), w]
    out_specs = [pl.BlockSpec((tm, tn), lambda i, j: (i, j))]
    out_shape = [jax.ShapeDtypeStruct((T, N), f32)]
    if n_extra:
        in_specs += [pl.BlockSpec((D, n_extra), lambda i, j: (0, 0)), pl.BlockSpec((1, n_extra), lambda i, j: (0, 0))]
        args += [wg, gb]
        out_specs.append(pl.BlockSpec((tm, n_extra), lambda i, j: (i, 0)))
        out_shape.append(jax.ShapeDtypeStruct((T, n_extra), f32))
    if emit_h:
        out_specs.append(pl.BlockSpec((tm, D), lambda i, j: (i, 0)))
        out_shape.append(jax.ShapeDtypeStruct((T, D), f32))
    return pl.pallas_call(
        functools.partial(_norm_mm_kernel, n_extra=n_extra, emit_h=emit_h),
        grid=(T // tm, N // tn),
        in_specs=in_specs,
        out_specs=out_specs,
        out_shape=out_shape,
        scratch_shapes=[pltpu.VMEM((tm, D), bf16)],
        compiler_params=_cparams("arbitrary", "arbitrary"),
        name="norm_mm",
    )(*args)


def _prep_kernel(zq_ref, zc_ref, zs_ref, zw_ref, cos_ref, sin_ref, qg_ref, kg_ref,
                 q_ref, kc_ref, vc_ref, ks_ref, vs_ref, kw_ref, vw_ref):
    cos = cos_ref[...]
    sin = sin_ref[...]

    def rope(y):
        return y * cos + pltpu.roll(y, HEAD_DIM // 2, 1) * sin

    def head(ref, h):
        return ref[:, h * HEAD_DIM:(h + 1) * HEAD_DIM]

    qg = qg_ref[...]
    for h in range(NSA_HEADS):
        q_ref[0, h] = rope(_rms(head(zq_ref, h), qg)).astype(bf16)
    for g in range(NSA_KV_HEADS):
        kc_ref[0, g] = rope(head(zc_ref, g))
        vc_ref[0, g] = head(zc_ref, NSA_KV_HEADS + g)
        ks_ref[0, g] = rope(_rms(head(zs_ref, g), kg_ref[1:2])).astype(bf16)
        vs_ref[0, g] = head(zs_ref, NSA_KV_HEADS + g).astype(bf16)
        kw_ref[0, g] = rope(_rms(head(zw_ref, g), kg_ref[2:3])).astype(bf16)
        vw_ref[0, g] = head(zw_ref, NSA_KV_HEADS + g).astype(bf16)


def _prep(z, cos, sin, q_g, k_g, B, S, *, ts=256):
    per_b = S // ts
    G = NSA_KV_HEADS
    kv = 2 * KV_COLS
    base = (NSA_WIDTH + 2 * RNN_WIDTH) // kv

    def zspec(width, idx):
        return pl.BlockSpec((ts, width), lambda b, i: (b * per_b + i, idx))

    def hm(nh, dt):
        return pl.BlockSpec((1, nh, ts, HEAD_DIM), lambda b, i: (b, 0, i, 0)), jax.ShapeDtypeStruct((B, nh, S, HEAD_DIM), dt)

    outs = [hm(NSA_HEADS, bf16), hm(G, f32), hm(G, f32), hm(G, bf16), hm(G, bf16), hm(G, bf16), hm(G, bf16)]
    return pl.pallas_call(
        _prep_kernel,
        grid=(B, per_b),
        in_specs=[
            zspec(NSA_WIDTH, 0), zspec(kv, base), zspec(kv, base + 1), zspec(kv, base + 2),
            pl.BlockSpec((ts, HEAD_DIM), lambda b, i: (i, 0)),
            pl.BlockSpec((ts, HEAD_DIM), lambda b, i: (i, 0)),
            pl.BlockSpec((1, HEAD_DIM), lambda b, i: (0, 0)),
            pl.BlockSpec((3, HEAD_DIM), lambda b, i: (0, 0)),
        ],
        out_specs=[o[0] for o in outs],
        out_shape=[o[1] for o in outs],
        compiler_params=_cparams("arbitrary", "arbitrary"),
        name="prep",
    )(z, z, z, z, cos, sin, q_g.reshape(1, HEAD_DIM), k_g)


def _compress_kernel(kseg_ref, vseg_ref, pek_ref, pev_ref, wk_ref, wv_ref, kg_ref, kc_ref, vc_ref, *, n_cmp):
    n_seg = kseg_ref.shape[2]
    half = CMP_STRIDE * HEAD_DIM
    row = lax.broadcasted_iota(i32, (n_seg, HEAD_DIM), 0)

    def comp(seg_ref, pe_ref, w_ref):
        seg = seg_ref[0, 0]
        lo = jnp.dot((seg + pe_ref[0:1]).astype(bf16), w_ref[0:half], preferred_element_type=f32)
        hi = jnp.dot((seg + pe_ref[1:2]).astype(bf16), w_ref[half:2 * half], preferred_element_type=f32)
        return lo + pltpu.roll(hi, n_seg - 1, 0)

    kc = _rms(comp(kseg_ref, pek_ref, wk_ref), kg_ref[...])
    vc = comp(vseg_ref, pev_ref, wv_ref)
    kc_ref[0, 0] = jnp.where(row < n_cmp, kc, 0.0).astype(bf16)
    vc_ref[0, 0] = jnp.where(row < n_cmp, vc, 0.0).astype(bf16)


def _compress(kc_r, vc_r, pe_k, pe_v, w_k, w_v, k_g0):
    B, G, S, dh = kc_r.shape
    n_seg = S // CMP_STRIDE
    n_cmp = (S - CMP_LEN) // CMP_STRIDE + 1
    half = CMP_STRIDE * dh
    seg = lambda a: a.reshape(B, G, n_seg, half)
    segspec = pl.BlockSpec((1, 1, n_seg, half), lambda b, g: (b, g, 0, 0))
    full = lambda shape: pl.BlockSpec(shape, lambda b, g: (0,) * len(shape))
    ospec = pl.BlockSpec((1, 1, n_seg, dh), lambda b, g: (b, g, 0, 0))
    return pl.pallas_call(
        functools.partial(_compress_kernel, n_cmp=n_cmp),
        grid=(B, G),
        in_specs=[segspec, segspec, full((2, half)), full((2, half)), full((2 * half, dh)), full((2 * half, dh)),
                  full((1, dh))],
        out_specs=[ospec, ospec],
        out_shape=[jax.ShapeDtypeStruct((B, G, n_seg, dh), bf16)] * 2,
        compiler_params=_cparams("arbitrary", "arbitrary"),
        name="compress",
    )(seg(kc_r), seg(vc_r), pe_k.reshape(2, half), pe_v.reshape(2, half), w_k.astype(bf16), w_v.astype(bf16),
      k_g0.reshape(1, dh))


def _attn_kernel(q_ref, kc_ref, vc_ref, ks_ref, vs_ref, kw_ref, vw_ref, gate_ref, ov_ref, ex_ref, o_ref,
                 *, tq, tk, n_cmp, n_slc, n_sel):
    R = NSA_GROUP
    scale = HEAD_DIM ** -0.5
    t0 = pl.program_id(2) * tq
    q = q_ref[0].reshape(R * tq, HEAD_DIM)
    row_t = t0 + lax.broadcasted_iota(i32, (tq, 1), 0)
    row_t4 = t0 + (lax.broadcasted_iota(i32, (R * tq, 1), 0) & (tq - 1))
    per_head = lambda a: jnp.concatenate([a] * R, axis=0)

    ncl = kc_ref.shape[2]
    cl = lax.broadcasted_iota(i32, (1, ncl), 1)
    cmp_end = jnp.where(cl < n_cmp, cl * CMP_STRIDE + (CMP_LEN - 1), jnp.int32(2 ** 30))
    s = lax.dot_general(q, kc_ref[0, 0], NT_DIMS, preferred_element_type=f32) * scale
    s = jnp.where(cmp_end <= row_t4, s, -jnp.inf)
    m = jnp.max(s, axis=-1, keepdims=True)
    m = jnp.where(m == -jnp.inf, 0.0, m)
    p = jnp.exp(s - m)
    d = jnp.sum(p, axis=-1, keepdims=True)
    p = p / jnp.where(d > 0, d, 1.0)
    o_cmp = jnp.dot(p.astype(bf16), vc_ref[0, 0], preferred_element_type=f32)

    psum = p[0:tq]
    for r in range(1, R):
        psum = psum + p[r * tq:(r + 1) * tq]
    p_hi = psum.astype(bf16)
    p_lo = (psum - p_hi.astype(f32)).astype(bf16)
    imp = jnp.dot(p_hi, ov_ref[...], preferred_element_type=f32) + jnp.dot(p_lo, ov_ref[...], preferred_element_type=f32)
    jb = lax.broadcasted_iota(i32, (1, LANES), 1)
    cur = row_t // SLC_LEN
    imp = jnp.where(jb * SLC_LEN <= row_t, imp, -FORCED_SCORE)
    imp = jnp.where(jb == 0, FORCED_SCORE, jnp.where(jb == cur, FORCED_SCORE, jnp.where(jb == cur - 1, FORCED_SCORE, imp)))
    imp = jnp.where(jb < n_slc, imp, -jnp.inf)
    rank = jnp.zeros((tq, LANES), f32)
    for i in range(n_slc):
        col = imp[:, i:i + 1]
        later = jnp.where(jb > i, 1.0, 0.0)
        rank = rank + jnp.where(col > imp, 1.0, jnp.where(col == imp, later, 0.0))
    sel = jnp.where(rank < n_sel, jnp.where(jb < n_slc, 1.0, 0.0), 0.0).astype(bf16)

    def slc_chunk(c, carry):
        m_i, l_i, acc = carry
        k0 = pl.multiple_of(c * tk, tk)
        k = ks_ref[0, 0, pl.ds(k0, tk), :]
        v = vs_ref[0, 0, pl.ds(k0, tk), :]
        sc = lax.dot_general(q, k, NT_DIMS, preferred_element_type=f32) * scale
        kpos = k0 + lax.broadcasted_iota(i32, (1, tk), 1)
        picked = per_head(jnp.dot(sel, ex_ref[c], preferred_element_type=f32))
        sc = jnp.where(kpos <= row_t4, jnp.where(picked > 0.5, sc, NEG_BIG), NEG_BIG)
        m_n = jnp.maximum(m_i, jnp.max(sc, axis=-1, keepdims=True))
        alpha = jnp.exp(m_i - m_n)
        pc = jnp.exp(sc - m_n)
        l_n = alpha * l_i + jnp.sum(pc, axis=-1, keepdims=True)
        acc_n = alpha * acc + jnp.dot(pc.astype(bf16), v, preferred_element_type=f32)
        return m_n, l_n, acc_n

    n_chunks = (t0 + tq + tk - 1) // tk
    init = (jnp.full((R * tq, 1), NEG_BIG, f32), jnp.zeros((R * tq, 1), f32), jnp.zeros((R * tq, HEAD_DIM), f32))
    _, l_s, acc_s = lax.fori_loop(0, n_chunks, slc_chunk, init)
    o_slc = acc_s / l_s

    wk = WINDOW + tq
    w0 = pl.multiple_of(jnp.maximum(t0 - WINDOW, 0), tq)
    kpos = w0 + lax.broadcasted_iota(i32, (1, wk), 1)
    sw = lax.dot_general(q, kw_ref[0, 0, pl.ds(w0, wk), :], NT_DIMS, preferred_element_type=f32) * scale
    sw = jnp.where(kpos <= row_t4, jnp.where(kpos > row_t4 - WINDOW, sw, NEG_BIG), NEG_BIG)
    pw = jnp.exp(sw - jnp.max(sw, axis=-1, keepdims=True))
    o_win = jnp.dot(pw.astype(bf16), vw_ref[0, 0, pl.ds(w0, wk), :], preferred_element_type=f32)
    o_win = o_win / jnp.sum(pw, axis=-1, keepdims=True)

    gates = gate_ref[...]
    for r in range(R):
        rows = slice(r * tq, (r + 1) * tq)
        o_ref[:, r * HEAD_DIM:(r + 1) * HEAD_DIM] = (
            gates[:, 3 * r:3 * r + 1] * o_cmp[rows]
            + gates[:, 3 * r + 1:3 * r + 2] * o_slc[rows]
            + gates[:, 3 * r + 2:3 * r + 3] * o_win[rows])


def _attention(q, kc, vc, ks, vs, kw, vw, gates, *, tq=128, tk=512):
    B, H, S, dh = q.shape
    G = NSA_KV_HEADS
    n_seg = kc.shape[2]
    n_cmp = (S - CMP_LEN) // CMP_STRIDE + 1
    n_slc = S // SLC_LEN
    n_sel = min(N_SELECT, n_slc)
    assert n_seg == LANES and n_slc <= LANES and S % tk == 0 and tk % tq == 0 and S >= WINDOW + tq
    per_b = S // tq
    cs = np.arange(n_seg)[:, None] * CMP_STRIDE
    ss = np.arange(LANES)[None, :] * SLC_LEN
    ov = np.maximum(np.minimum(cs + CMP_LEN, ss + SLC_LEN) - np.maximum(cs, ss), 0).astype(np.float32) / CMP_LEN
    ov[n_cmp:, :] = 0.0
    ov[:, n_slc:] = 0.0
    kpos = np.arange(S).reshape(S // tk, 1, tk)
    ex = (kpos // SLC_LEN == np.arange(LANES)[None, :, None]).astype(np.float32)
    kvspec = pl.BlockSpec((1, 1, S, dh), lambda b, g, i: (b, g, 0, 0))
    cspec = pl.BlockSpec((1, 1, n_seg, dh), lambda b, g, i: (b, g, 0, 0))
    return pl.pallas_call(
        functools.partial(_attn_kernel, tq=tq, tk=tk, n_cmp=n_cmp, n_slc=n_slc, n_sel=n_sel),
        grid=(B, G, per_b),
        in_specs=[
            pl.BlockSpec((1, NSA_GROUP, tq, dh), lambda b, g, i: (b, g, i, 0)),
            cspec, cspec, kvspec, kvspec, kvspec, kvspec,
            pl.BlockSpec((tq, LANES), lambda b, g, i: (b * per_b + i, g)),
            pl.BlockSpec((n_seg, LANES), lambda b, g, i: (0, 0)),
            pl.BlockSpec((S // tk, LANES, tk), lambda b, g, i: (0, 0, 0)),
        ],
        out_specs=pl.BlockSpec((tq, NSA_GROUP * dh), lambda b, g, i: (b * per_b + i, g)),
        out_shape=jax.ShapeDtypeStruct((B * S, NSA_WIDTH), f32),
        compiler_params=_cparams("arbitrary", "arbitrary", "arbitrary"),
        name="nsa_attention",
    )(q, kc, vc, ks, vs, kw, vw, gates, jnp.asarray(ov, bf16), jnp.asarray(ex, bf16))


def _gelu(x):
    return 0.5 * x * (1.0 + lax.erf(x * (2.0 ** -0.5)))


def _rglru_kernel(xr_ref, xg_ref, cw_ref, cb_ref, wa_ref, ba_ref, wi_ref, bi_ref, lam_ref, o_ref,
                  ext_ref, a_ref, b_ref, hs_ref, hc_ref, *, ts):
    halo = 8

    @pl.when(pl.program_id(1) == 0)
    def _():
        ext_ref[0:halo] = jnp.zeros((halo, RNN_WIDTH), f32)
        hc_ref[...] = jnp.zeros_like(hc_ref)

    xr = xr_ref[...]
    ext_ref[halo:halo + ts] = xr
    u = cb_ref[...] + cw_ref[CONV_WIDTH - 1:CONV_WIDTH] * xr
    for k in range(1, CONV_WIDTH):
        u = u + cw_ref[CONV_WIDTH - 1 - k:CONV_WIDTH - k] * ext_ref[halo - k:halo - k + ts]
    ext_ref[0:halo] = xr[ts - halo:ts]

    ra, ri = [], []
    for n in range(RNN_BLOCKS):
        ub = u[:, n * RNN_BLOCK_DIM:(n + 1) * RNN_BLOCK_DIM].astype(bf16)
        ra.append(jnp.dot(ub, wa_ref[n], preferred_element_type=f32))
        ri.append(jnp.dot(ub, wi_ref[n], preferred_element_type=f32))
    r = jax.nn.sigmoid(jnp.concatenate(ra, axis=1) + ba_ref[...])
    ig = jax.nn.sigmoid(jnp.concatenate(ri, axis=1) + bi_ref[...])
    log_a = (-LRU_C) * r * jax.nn.softplus(-lam_ref[...])
    a_ref[...] = jnp.exp(log_a)
    b_ref[...] = jnp.sqrt(1.0 - jnp.exp(2.0 * log_a)) * ig * u

    sub = lax.broadcasted_iota(i32, (8, RNN_WIDTH), 0)

    def blk(j, h):
        r0 = pl.multiple_of(j * 8, 8)
        a8 = a_ref[pl.ds(r0, 8), :]
        b8 = b_ref[pl.ds(r0, 8), :]
        out = jnp.zeros((8, RNN_WIDTH), f32)
        for i in range(8):
            h = a8[i:i + 1] * h + b8[i:i + 1]
            out = jnp.where(sub == i, h, out)
        hs_ref[pl.ds(r0, 8), :] = out
        return h

    hc_ref[...] = lax.fori_loop(0, ts // 8, blk, hc_ref[...])
    o_ref[...] = _gelu(xg_ref[...]) * hs_ref[...]


def _rglru(z, conv_w, conv_b, wa, ba, wi, bi, lam, B, S, *, ts=512):
    W = RNN_WIDTH
    per_b = S // ts
    xr_idx = NSA_WIDTH // W
    vec = lambda a: a.reshape(1, W)
    full = lambda shape: pl.BlockSpec(shape, lambda b, i: (0,) * len(shape))
    return pl.pallas_call(
        functools.partial(_rglru_kernel, ts=ts),
        grid=(B, per_b),
        in_specs=[
            pl.BlockSpec((ts, W), lambda b, i: (b * per_b + i, xr_idx)),
            pl.BlockSpec((ts, W), lambda b, i: (b * per_b + i, xr_idx + 1)),
            full((CONV_WIDTH, W)), full((1, W)),
            full((RNN_BLOCKS, RNN_BLOCK_DIM, RNN_BLOCK_DIM)), full((1, W)),
            full((RNN_BLOCKS, RNN_BLOCK_DIM, RNN_BLOCK_DIM)), full((1, W)), full((1, W)),
        ],
        out_specs=pl.BlockSpec((ts, W), lambda b, i: (b * per_b + i, 0)),
        out_shape=jax.ShapeDtypeStruct((B * S, W), f32),
        scratch_shapes=[pltpu.VMEM((ts + 8, W), f32), pltpu.VMEM((ts, W), f32), pltpu.VMEM((ts, W), f32),
                        pltpu.VMEM((ts, W), f32), pltpu.VMEM((1, W), f32)],
        compiler_params=_cparams("arbitrary", "arbitrary"),
        name="rglru",
    )(z, z, conv_w, vec(conv_b), wa.astype(bf16), vec(ba), wi.astype(bf16), vec(bi), vec(lam))


def _outproj_kernel(oa_ref, or_ref, ga_ref, gr_ref, w_ref, x_ref, gt_ref, o_ref, yb_ref):
    @pl.when(pl.program_id(1) == 0)
    def _():
        wa = oa_ref.shape[1]
        yb_ref[:, 0:wa] = _rms(oa_ref[...], ga_ref[...]).astype(bf16)
        yb_ref[:, wa:] = _rms(or_ref[...], gr_ref[...]).astype(bf16)

    o_ref[...] = x_ref[...] + gt_ref[0] * jnp.dot(yb_ref[...], w_ref[...], preferred_element_type=f32)


def _outproj(o_attn, o_rnn, g_a, g_r, w, x, gt, seq, *, tm=512, tn=1024):
    T, D = x.shape
    B = gt.shape[0]
    wa, wr = o_attn.shape[1], o_rnn.shape[1]
    per_b = seq // tm
    return pl.pallas_call(
        _outproj_kernel,
        grid=(T // tm, D // tn),
        in_specs=[
            pl.BlockSpec((tm, wa), lambda i, j: (i, 0)),
            pl.BlockSpec((tm, wr), lambda i, j: (i, 0)),
            pl.BlockSpec((1, wa), lambda i, j: (0, 0)),
            pl.BlockSpec((1, wr), lambda i, j: (0, 0)),
            pl.BlockSpec((wa + wr, tn), lambda i, j: (0, j)),
            pl.BlockSpec((tm, tn), lambda i, j: (i, j)),
            pl.BlockSpec((1, 1, tn), lambda i, j: (i // per_b, 0, j)),
        ],
        out_specs=pl.BlockSpec((tm, tn), lambda i, j: (i, j)),
        out_shape=jax.ShapeDtypeStruct((T, D), f32),
        scratch_shapes=[pltpu.VMEM((tm, wa + wr), bf16)],
        compiler_params=_cparams("arbitrary", "arbitrary"),
        name="out_proj",
    )(o_attn, o_rnn, g_a.reshape(1, wa), g_r.reshape(1, wr), w, x, gt.reshape(B, 1, D))


def _top_rows(s, k):
    n = s.shape[0]
    io = lax.broadcasted_iota(i32, s.shape, 0)
    vals, idxs = [], []
    for _ in range(k):
        m = jnp.max(s, axis=0, keepdims=True)
        ix = jnp.min(jnp.where(s == m, io, n), axis=0, keepdims=True)
        vals.append(m)
        idxs.append(ix)
        s = jnp.where(io == ix, -jnp.inf, s)
    return vals, idxs


def _peer_topk_kernel(q_ref, keys_ref, e_ref, g_ref):
    K = PEER_TOPK
    tt = q_ref.shape[0]
    pairs = [(a, b) for a in range(K) for b in range(K) if (a + 1) * (b + 1) <= K]
    n_cand = len(pairs) + (-len(pairs) % 8)
    cio = lax.broadcasted_iota(i32, (n_cand, tt), 0)
    oio = lax.broadcasted_iota(i32, (PEER_HEADS * K, tt), 0)
    e_all = jnp.zeros((PEER_HEADS * K, tt), i32)
    g_all = jnp.zeros((PEER_HEADS * K, tt), f32)
    for h in range(PEER_HEADS):
        tops = []
        for p in range(2):
            c0 = (2 * h + p) * PEER_KEYS
            qhp = q_ref[:, c0:c0 + PEER_KEYS].astype(bf16)
            s = lax.dot_general(keys_ref[h, p], qhp, NT_DIMS, preferred_element_type=f32)
            tops.append(_top_rows(s, K))
        (v1, i1), (v2, i2) = tops
        cand = jnp.full((n_cand, tt), -jnp.inf, f32)
        cidx = jnp.zeros((n_cand, tt), i32)
        for r, (a, b) in enumerate(pairs):
            cand = jnp.where(cio == r, v1[a] + v2[b], cand)
            cidx = jnp.where(cio == r, i1[a] * PEER_KEYS + i2[b], cidx)
        vals, pos = _top_rows(cand, K)
        ex = [jnp.exp(v - vals[0]) for v in vals]
        den = functools.reduce(lambda x, y: x + y, ex)
        for r in range(K):
            e_r = jnp.sum(jnp.where(cio == pos[r], cidx, 0), axis=0, keepdims=True)
            e_all = jnp.where(oio == h * K + r, e_r, e_all)
            g_all = jnp.where(oio == h * K + r, ex[r] / den, g_all)
    e_ref[...] = e_all.T
    g_ref[...] = g_all.T


def _peer_topk(qp, keys, *, tt=256):
    T, N = qp.shape
    HK = PEER_HEADS * PEER_TOPK
    return pl.pallas_call(
        _peer_topk_kernel,
        grid=(T // tt,),
        in_specs=[pl.BlockSpec((tt, N), lambda i: (i, 0)),
                  pl.BlockSpec(keys.shape, lambda i: (0, 0, 0, 0))],
        out_specs=[pl.BlockSpec((tt, HK), lambda i: (i, 0))] * 2,
        out_shape=[jax.ShapeDtypeStruct((T, HK), i32), jax.ShapeDtypeStruct((T, HK), f32)],
        compiler_params=_cparams("arbitrary"),
        name="peer_topk",
    )(qp, keys.astype(bf16))


def _split_dot_ones(x):
    ones = jnp.ones((LANES, LANES), bf16)
    hi = x.astype(bf16)
    lo = (x - hi.astype(f32)).astype(bf16)
    return jnp.dot(hi, ones, preferred_element_type=f32) + jnp.dot(lo, ones, preferred_element_type=f32)


def _peer_ffn_kernel(idx_ref, g_ref, h_ref, x_ref, gt_ref, tab_ref, o_ref, buf_ref, sem_ref, *, tt):
    NE = idx_ref.shape[1]
    D = h_ref.shape[1]
    n_lt = D // LANES

    def issue(t, slot):
        for k in range(NE):
            pltpu.make_async_copy(tab_ref.at[idx_ref[t, k]], buf_ref.at[slot, k], sem_ref.at[slot]).start(priority=k % 2)

    def wait(slot):
        pltpu.make_async_copy(tab_ref.at[pl.ds(0, NE)], buf_ref.at[slot], sem_ref.at[slot]).wait()

    diag = lax.broadcasted_iota(i32, (NE, LANES), 0) == lax.broadcasted_iota(i32, (NE, LANES), 1)
    ones8 = jnp.ones((8, LANES), bf16)
    issue(0, 0)

    def token(t, carry):
        slot = t % 2

        @pl.when(t + 1 < tt)
        def _():
            issue(t + 1, 1 - slot)

        wait(slot)
        packed = buf_ref[slot]
        xrow = h_ref[pl.ds(t, 1), :]
        part = jnp.zeros((NE, LANES), f32)
        for c in range(n_lt):
            cols = slice(c * LANES, (c + 1) * LANES)
            part = part + pltpu.bitcast(packed[:, cols] << 16, f32) * xrow[:, cols]
        p_hi = part.astype(bf16)
        p_lo = (part - p_hi.astype(f32)).astype(bf16)
        act = (lax.dot_general(ones8, p_hi, NT_DIMS, preferred_element_type=f32)
               + lax.dot_general(ones8, p_lo, NT_DIMS, preferred_element_type=f32))[0:1]
        w_row = _gelu(act) * g_ref[pl.ds(t, 1), :]
        w_col = _split_dot_ones(jnp.where(diag, w_row, 0.0))
        up = pltpu.bitcast(packed & jnp.uint32(0xFFFF0000), f32)
        outs = [jnp.sum(up[:, c * LANES:(c + 1) * LANES] * w_col, axis=0, keepdims=True) for c in range(n_lt)]
        y = jnp.concatenate(outs, axis=1)
        o_ref[pl.ds(t, 1), :] = x_ref[pl.ds(t, 1), :] + gt_ref[0] * y
        return carry

    lax.fori_loop(0, tt, token, 0)


def _peer_ffn(eidx, gsm, h2, x1, gt2, table, seq, *, tt=32):
    T, D = h2.shape
    B = gt2.shape[0]
    NE = eidx.shape[1]
    per_b = seq // tt
    return pl.pallas_call(
        functools.partial(_peer_ffn_kernel, tt=tt),
        grid=(T // tt,),
        in_specs=[
            pl.BlockSpec((tt, NE), lambda i: (i, 0), memory_space=pltpu.SMEM),
            pl.BlockSpec((tt, NE), lambda i: (i, 0)),
            pl.BlockSpec((tt, D), lambda i: (i, 0)),
            pl.BlockSpec((tt, D), lambda i: (i, 0)),
            pl.BlockSpec((1, 1, D), lambda i: (i // per_b, 0, 0)),
            pl.BlockSpec(memory_space=pl.ANY),
        ],
        out_specs=pl.BlockSpec((tt, D), lambda i: (i, 0)),
        out_shape=jax.ShapeDtypeStruct((T, D), f32),
        scratch_shapes=[pltpu.VMEM((2, NE, D), jnp.uint32), pltpu.SemaphoreType.DMA((2,))],
        compiler_params=_cparams("arbitrary"),
        name="peer_ffn",
    )(eidx, gsm, h2, x1, gt2.reshape(B, 1, D), table)


def _pack_experts(down, up):
    lo = lax.bitcast_convert_type(down.astype(bf16), jnp.uint16).astype(jnp.uint32)
    hi = lax.bitcast_convert_type(up.astype(bf16), jnp.uint16).astype(jnp.uint32)
    return lo | (hi << 16)


def _rope_tables(S):
    half = HEAD_DIM // 2
    freqs = ROPE_THETA ** (-jnp.arange(half, dtype=f32) / half)
    ang = jnp.arange(S, dtype=f32)[:, None] * freqs[None, :]
    cos, sin = jnp.cos(ang), jnp.sin(ang)
    return jnp.concatenate([cos, cos], axis=1), jnp.concatenate([-sin, sin], axis=1)


def _layer(x, c, ada_w, ada_b, norm_mix_g, norm_ffn_g, w_in, w_out, q_norm_g, k_norm_g, cmp_pe_k, cmp_pe_v,
           cmp_w_k, cmp_w_v, gate_b, conv_w, conv_b, lru_wa, lru_ba, lru_wi, lru_bi, lru_lam, out_g_attn,
           out_g_rnn, peer_wq, peer_keys, peer_down, peer_up):
    B, S, D = x.shape
    T = B * S
    xf = x.reshape(T, D)
    mod = _mod(c, ada_w, ada_b)
    sh1, sc1, gt1, sh2, sc2, gt2 = [mod[:, k * D:(k + 1) * D] for k in range(6)]

    q0, g0 = NSA_WIDTH, NSA_WIDTH + 6 * KV_COLS
    r0 = g0 + GATE_COLS
    kvs = [w_in[:, q0 + k * KV_COLS:q0 + (k + 1) * KV_COLS] for k in range(6)]
    w_main = jnp.concatenate([w_in[:, :q0], w_in[:, r0:r0 + 2 * RNN_WIDTH]] + kvs, axis=1).astype(bf16)
    per_g = GATE_COLS // NSA_KV_HEADS
    wg = jnp.zeros((D, NSA_KV_HEADS * LANES), f32)
    gb = jnp.zeros((1, NSA_KV_HEADS * LANES), f32)
    for g in range(NSA_KV_HEADS):
        wg = wg.at[:, g * LANES:g * LANES + per_g].set(w_in[:, g0 + g * per_g:g0 + (g + 1) * per_g])
        gb = gb.at[0, g * LANES:g * LANES + per_g].set(gate_b[g * per_g:(g + 1) * per_g])
    z, gates = _norm_mm(xf, sc1, sh1, norm_mix_g, w_main, S, tm=512, tn=768, wg=wg.astype(bf16), gb=gb)

    cos, sin = _rope_tables(S)
    q, kc_r, vc_r, ks, vs, kw, vw = _prep(z, cos, sin, q_norm_g, k_norm_g, B, S)
    kc, vc = _compress(kc_r, vc_r, cmp_pe_k, cmp_pe_v, cmp_w_k, cmp_w_v, k_norm_g[0])
    o_attn = _attention(q, kc, vc, ks, vs, kw, vw, gates)
    o_rnn = _rglru(z, conv_w, conv_b, lru_wa, lru_ba, lru_wi, lru_bi, lru_lam, B, S)
    x1 = _outproj(o_attn, o_rnn, out_g_attn, out_g_rnn, w_out.astype(bf16), xf, gt1, S)

    qp, h2 = _norm_mm(x1, sc2, sh2, norm_ffn_g, peer_wq.astype(bf16), S, tm=512, tn=1024, emit_h=True)
    eidx, gsm = _peer_topk(qp, peer_keys)
    out = _peer_ffn(eidx, gsm, h2, x1, gt2, _pack_experts(peer_down, peer_up), S)
    return out.reshape(B, S, D)


def kernel(x, c, ada_w, ada_b, norm_mix_g, norm_ffn_g, w_in, w_out, q_norm_g, k_norm_g, cmp_pe_k, cmp_pe_v, cmp_w_k, cmp_w_v, gate_b, conv_w, conv_b, lru_wa, lru_ba, lru_wi, lru_bi, lru_lam, out_g_attn, out_g_rnn, peer_wq, peer_keys, peer_down, peer_up):
    params = (ada_w, ada_b, norm_mix_g, norm_ffn_g, w_in, w_out, q_norm_g, k_norm_g, cmp_pe_k, cmp_pe_v, cmp_w_k,
              cmp_w_v, gate_b, conv_w, conv_b, lru_wa, lru_ba, lru_wi, lru_bi, lru_lam, out_g_attn, out_g_rnn,
              peer_wq, peer_keys, peer_down, peer_up)
    for l in range(ada_w.shape[0]):
        x = _layer(x, c, *[p[l] for p in params])
    return x
```

```python
import functools

import numpy as np
import jax
import jax.numpy as jnp
from jax import lax
from jax.experimental import pallas as pl
from jax.experimental.pallas import tpu as pltpu

f32 = jnp.float32
bf16 = jnp.bfloat16
i32 = jnp.int32

HEAD_DIM = 128
NSA_HEADS = 8
NSA_KV_HEADS = 2
NSA_GROUP = NSA_HEADS // NSA_KV_HEADS
NSA_WIDTH = NSA_HEADS * HEAD_DIM
KV_COLS = NSA_KV_HEADS * HEAD_DIM
GATE_COLS = 3 * NSA_HEADS
CMP_LEN = 32
CMP_STRIDE = 16
SLC_LEN = 64
N_SELECT = 16
FORCED_SCORE = 1e4
WINDOW = 512
RNN_WIDTH = 1024
RNN_BLOCKS = 8
RNN_BLOCK_DIM = RNN_WIDTH // RNN_BLOCKS
CONV_WIDTH = 4
LRU_C = 8.0
PEER_HEADS = 8
PEER_KEYS = 128
PEER_TOPK = 16
ROPE_THETA = 10000.0
EPS = 1e-6
NEG_BIG = -1e30

LANES = 128
VMEM_LIMIT = 56 * 1024 * 1024

NT_DIMS = (((1,), (1,)), ((), ()))


def _cparams(*sem):
    return pltpu.CompilerParams(dimension_semantics=sem, vmem_limit_bytes=VMEM_LIMIT)


def _rms(x, g):
    return x * lax.rsqrt(jnp.mean(x * x, axis=-1, keepdims=True) + EPS) * g


def _mod_kernel(c_ref, w_ref, b_ref, o_ref):
    c = c_ref[...]
    a = c * jax.nn.sigmoid(c)
    o_ref[...] = jnp.dot(a, w_ref[...], preferred_element_type=f32) + b_ref[...]


def _mod(c, w, b, tn=1024):
    B, D = c.shape
    N = w.shape[1]
    return pl.pallas_call(
        _mod_kernel,
        grid=(N // tn,),
        in_specs=[
            pl.BlockSpec((B, D), lambda j: (0, 0)),
            pl.BlockSpec((D, tn), lambda j: (0, j)),
            pl.BlockSpec((1, tn), lambda j: (0, j)),
        ],
        out_specs=pl.BlockSpec((B, tn), lambda j: (0, j)),
        out_shape=jax.ShapeDtypeStruct((B, N), f32),
        compiler_params=_cparams("arbitrary"),
        name="mod",
    )(c, w, b.reshape(1, N))


def _norm_mm_kernel(x_ref, sc_ref, sh_ref, g_ref, w_ref, *rest, n_extra, emit_h):
    rest = list(rest)
    if n_extra:
        wg_ref, gb_ref = rest.pop(0), rest.pop(0)
    z_ref = rest.pop(0)
    if n_extra:
        gate_ref = rest.pop(0)
    if emit_h:
        h_ref = rest.pop(0)
    hb_ref = rest.pop(0)

    @pl.when(pl.program_id(1) == 0)
    def _():
        h = _rms(x_ref[...], g_ref[...]) * (1.0 + sc_ref[0]) + sh_ref[0]
        hb = h.astype(bf16)
        hb_ref[...] = hb
        if emit_h:
            h_ref[...] = h
        if n_extra:
            gl = jnp.dot(hb, wg_ref[...], preferred_element_type=f32)
            gate_ref[...] = jax.nn.sigmoid(gl + gb_ref[...])

    z_ref[...] = jnp.dot(hb_ref[...], w_ref[...], preferred_element_type=f32)


def _norm_mm(x, sc, sh, g, w, seq, *, tm, tn, wg=None, gb=None, emit_h=False):
    T, D = x.shape
    N = w.shape[1]
    B = sc.shape[0]
    assert seq % tm == 0 and N % tn == 0
    per_b = seq // tm
    n_extra = 0 if wg is None else wg.shape[1]
    in_specs = [
        pl.BlockSpec((tm, D), lambda i, j: (i, 0)),
        pl.BlockSpec((1, 1, D), lambda i, j: (i // per_b, 0, 0)),
        pl.BlockSpec((1, 1, D), lambda i, j: (i // per_b, 0, 0)),
        pl.BlockSpec((1, D), lambda i, j: (0, 0)),
        pl.BlockSpec((D, tn), lambda i, j: (0, j)),
    ]
    args = [x, sc.reshape(B, 1, D), sh.reshape(B, 1, D), g.reshape(1, D), w]
    out_specs = [pl.BlockSpec((tm, tn), lambda i, j: (i, j))]
    out_shape = [jax.ShapeDtypeStruct((T, N), f32)]
    if n_extra:
        in_specs += [pl.BlockSpec((D, n_extra), lambda i, j: (0, 0)), pl.BlockSpec((1, n_extra), lambda i, j: (0, 0))]
        args += [wg, gb]
        out_specs.append(pl.BlockSpec((tm, n_extra), lambda i, j: (i, 0)))
        out_shape.append(jax.ShapeDtypeStruct((T, n_extra), f32))
    if emit_h:
        out_specs.append(pl.BlockSpec((tm, D), lambda i, j: (i, 0)))
        out_shape.append(jax.ShapeDtypeStruct((T, D), f32))
    return pl.pallas_call(
        functools.partial(_norm_mm_kernel, n_extra=n_extra, emit_h=emit_h),
        grid=(T // tm, N // tn),
        in_specs=in_specs,
        out_specs=out_specs,
        out_shape=out_shape,
        scratch_shapes=[pltpu.VMEM((tm, D), bf16)],
        compiler_params=_cparams("arbitrary", "arbitrary"),
        name="norm_mm",
    )(*args)


def _prep_kernel(zq_ref, zc_ref, zs_ref, zw_ref, cos_ref, sin_ref, qg_ref, kg_ref,
                 q_ref, kc_ref, vc_ref, ks_ref, vs_ref, kw_ref, vw_ref):
    cos = cos_ref[...]
    sin = sin_ref[...]

    def rope(y):
        return y * cos + pltpu.roll(y, HEAD_DIM // 2, 1) * sin

    def head(ref, h):
        return ref[:, h * HEAD_DIM:(h + 1) * HEAD_DIM]

    qg = qg_ref[...]
    for h in range(NSA_HEADS):
        q_ref[0, h] = rope(_rms(head(zq_ref, h), qg)).astype(bf16)
    for g in range(NSA_KV_HEADS):
        kc_ref[0, g] = rope(head(zc_ref, g))
        vc_ref[0, g] = head(zc_ref, NSA_KV_HEADS + g)
        ks_ref[0, g] = rope(_rms(head(zs_ref, g), kg_ref[1:2])).astype(bf16)
        vs_ref[0, g] = head(zs_ref, NSA_KV_HEADS + g).astype(bf16)
        kw_ref[0, g] = rope(_rms(head(zw_ref, g), kg_ref[2:3])).astype(bf16)
        vw_ref[0, g] = head(zw_ref, NSA_KV_HEADS + g).astype(bf16)


def _prep(z, cos, sin, q_g, k_g, B, S, *, ts=256):
    per_b = S // ts
    G = NSA_KV_HEADS
    kv = 2 * KV_COLS
    base = (NSA_WIDTH + 2 * RNN_WIDTH) // kv

    def zspec(width, idx):
        return pl.BlockSpec((ts, width), lambda b, i: (b * per_b + i, idx))

    def hm(nh, dt):
        return pl.BlockSpec((1, nh, ts, HEAD_DIM), lambda b, i: (b, 0, i, 0)), jax.ShapeDtypeStruct((B, nh, S, HEAD_DIM), dt)

    outs = [hm(NSA_HEADS, bf16), hm(G, f32), hm(G, f32), hm(G, bf16), hm(G, bf16), hm(G, bf16), hm(G, bf16)]
    return pl.pallas_call(
        _prep_kernel,
        grid=(B, per_b),
        in_specs=[
            zspec(NSA_WIDTH, 0), zspec(kv, base), zspec(kv, base + 1), zspec(kv, base + 2),
            pl.BlockSpec((ts, HEAD_DIM), lambda b, i: (i, 0)),
            pl.BlockSpec((ts, HEAD_DIM), lambda b, i: (i, 0)),
            pl.BlockSpec((1, HEAD_DIM), lambda b, i: (0, 0)),
            pl.BlockSpec((3, HEAD_DIM), lambda b, i: (0, 0)),
        ],
        out_specs=[o[0] for o in outs],
        out_shape=[o[1] for o in outs],
        compiler_params=_cparams("arbitrary", "arbitrary"),
        name="prep",
    )(z, z, z, z, cos, sin, q_g.reshape(1, HEAD_DIM), k_g)


def _compress_kernel(kseg_ref, vseg_ref, pek_ref, pev_ref, wk_ref, wv_ref, kg_ref, kc_ref, vc_ref, *, n_cmp):
    n_seg = kseg_ref.shape[2]
    half = CMP_STRIDE * HEAD_DIM
    row = lax.broadcasted_iota(i32, (n_seg, HEAD_DIM), 0)

    def comp(seg_ref, pe_ref, w_ref):
        seg = seg_ref[0, 0]
        lo = jnp.dot((seg + pe_ref[0:1]).astype(bf16), w_ref[0:half], preferred_element_type=f32)
        hi = jnp.dot((seg + pe_ref[1:2]).astype(bf16), w_ref[half:2 * half], preferred_element_type=f32)
        return lo + pltpu.roll(hi, n_seg - 1, 0)

    kc = _rms(comp(kseg_ref, pek_ref, wk_ref), kg_ref[...])
    vc = comp(vseg_ref, pev_ref, wv_ref)
    kc_ref[0, 0] = jnp.where(row < n_cmp, kc, 0.0).astype(bf16)
    vc_ref[0, 0] = jnp.where(row < n_cmp, vc, 0.0).astype(bf16)


def _compress(kc_r, vc_r, pe_k, pe_v, w_k, w_v, k_g0):
    B, G, S, dh = kc_r.shape
    n_seg = S // CMP_STRIDE
    n_cmp = (S - CMP_LEN) // CMP_STRIDE + 1
    half = CMP_STRIDE * dh
    seg = lambda a: a.reshape(B, G, n_seg, half)
    segspec = pl.BlockSpec((1, 1, n_seg, half), lambda b, g: (b, g, 0, 0))
    full = lambda shape: pl.BlockSpec(shape, lambda b, g: (0,) * len(shape))
    ospec = pl.BlockSpec((1, 1, n_seg, dh), lambda b, g: (b, g, 0, 0))
    return pl.pallas_call(
        functools.partial(_compress_kernel, n_cmp=n_cmp),
        grid=(B, G),
        in_specs=[segspec, segspec, full((2, half)), full((2, half)), full((2 * half, dh)), full((2 * half, dh)),
                  full((1, dh))],
        out_specs=[ospec, ospec],
        out_shape=[jax.ShapeDtypeStruct((B, G, n_seg, dh), bf16)] * 2,
        compiler_params=_cparams("arbitrary", "arbitrary"),
        name="compress",
    )(seg(kc_r), seg(vc_r), pe_k.reshape(2, half), pe_v.reshape(2, half), w_k.astype(bf16), w_v.astype(bf16),
      k_g0.reshape(1, dh))


def _attn_kernel(q_ref, kc_ref, vc_ref, ks_ref, vs_ref, kw_ref, vw_ref, gate_ref, ov_ref, ex_ref, o_ref,
                 *, tq, tk, n_cmp, n_slc, n_sel):
    R = NSA_GROUP
    scale = HEAD_DIM ** -0.5
    t0 = pl.program_id(2) * tq
    q = q_ref[0].reshape(R * tq, HEAD_DIM)
    row_t = t0 + lax.broadcasted_iota(i32, (tq, 1), 0)
    row_t4 = t0 + (lax.broadcasted_iota(i32, (R * tq, 1), 0) & (tq - 1))
    per_head = lambda a: jnp.concatenate([a] * R, axis=0)

    ncl = kc_ref.shape[2]
    cl = lax.broadcasted_iota(i32, (1, ncl), 1)
    cmp_end = jnp.where(cl < n_cmp, cl * CMP_STRIDE + (CMP_LEN - 1), jnp.int32(2 ** 30))
    s = lax.dot_general(q, kc_ref[0, 0], NT_DIMS, preferred_element_type=f32) * scale
    s = jnp.where(cmp_end <= row_t4, s, -jnp.inf)
    m = jnp.max(s, axis=-1, keepdims=True)
    m = jnp.where(m == -jnp.inf, 0.0, m)
    p = jnp.exp(s - m)
    d = jnp.sum(p, axis=-1, keepdims=True)
    p = p / jnp.where(d > 0, d, 1.0)
    o_cmp = jnp.dot(p.astype(bf16), vc_ref[0, 0], preferred_element_type=f32)

    psum = p[0:tq]
    for r in range(1, R):
        psum = psum + p[r * tq:(r + 1) * tq]
    p_hi = psum.astype(bf16)
    p_lo = (psum - p_hi.astype(f32)).astype(bf16)
    imp = jnp.dot(p_hi, ov_ref[...], preferred_element_type=f32) + jnp.dot(p_lo, ov_ref[...], preferred_element_type=f32)
    jb = lax.broadcasted_iota(i32, (1, LANES), 1)
    cur = row_t // SLC_LEN
    imp = jnp.where(jb * SLC_LEN <= row_t, imp, -FORCED_SCORE)
    imp = jnp.where(jb == 0, FORCED_SCORE, jnp.where(jb == cur, FORCED_SCORE, jnp.where(jb == cur - 1, FORCED_SCORE, imp)))
    imp = jnp.where(jb < n_slc, imp, -jnp.inf)
    rank = jnp.zeros((tq, LANES), f32)
    for i in range(n_slc):
        col = imp[:, i:i + 1]
        later = jnp.where(jb > i, 1.0, 0.0)
        rank = rank + jnp.where(col > imp, 1.0, jnp.where(col == imp, later, 0.0))
    sel = jnp.where(rank < n_sel, jnp.where(jb < n_slc, 1.0, 0.0), 0.0).astype(bf16)

    def slc_chunk(c, carry):
        m_i, l_i, acc = carry
        k0 = pl.multiple_of(c * tk, tk)
        k = ks_ref[0, 0, pl.ds(k0, tk), :]
        v = vs_ref[0, 0, pl.ds(k0, tk), :]
        sc = lax.dot_general(q, k, NT_DIMS, preferred_element_type=f32) * scale
        kpos = k0 + lax.broadcasted_iota(i32, (1, tk), 1)
        picked = per_head(jnp.dot(sel, ex_ref[c], preferred_element_type=f32))
        sc = jnp.where(kpos <= row_t4, jnp.where(picked > 0.5, sc, NEG_BIG), NEG_BIG)
        m_n = jnp.maximum(m_i, jnp.max(sc, axis=-1, keepdims=True))
        alpha = jnp.exp(m_i - m_n)
        pc = jnp.exp(sc - m_n)
        l_n = alpha * l_i + jnp.sum(pc, axis=-1, keepdims=True)
        acc_n = alpha * acc + jnp.dot(pc.astype(bf16), v, preferred_element_type=f32)
        return m_n, l_n, acc_n

    n_chunks = (t0 + tq + tk - 1) // tk
    init = (jnp.full((R * tq, 1), NEG_BIG, f32), jnp.zeros((R * tq, 1), f32), jnp.zeros((R * tq, HEAD_DIM), f32))
    _, l_s, acc_s = lax.fori_loop(0, n_chunks, slc_chunk, init)
    o_slc = acc_s / l_s

    wk = WINDOW + tq
    w0 = pl.multiple_of(jnp.maximum(t0 - WINDOW, 0), tq)
    kpos = w0 + lax.broadcasted_iota(i32, (1, wk), 1)
    sw = lax.dot_general(q, kw_ref[0, 0, pl.ds(w0, wk), :], NT_DIMS, preferred_element_type=f32) * scale
    sw = jnp.where(kpos <= row_t4, jnp.where(kpos > row_t4 - WINDOW, sw, NEG_BIG), NEG_BIG)
    pw = jnp.exp(sw - jnp.max(sw, axis=-1, keepdims=True))
    o_win = jnp.dot(pw.astype(bf16), vw_ref[0, 0, pl.ds(w0, wk), :], preferred_element_type=f32)
    o_win = o_win / jnp.sum(pw, axis=-1, keepdims=True)

    gates = gate_ref[...]
    for r in range(R):
        rows = slice(r * tq, (r + 1) * tq)
        o_ref[:, r * HEAD_DIM:(r + 1) * HEAD_DIM] = (
            gates[:, 3 * r:3 * r + 1] * o_cmp[rows]
            + gates[:, 3 * r + 1:3 * r + 2] * o_slc[rows]
            + gates[:, 3 * r + 2:3 * r + 3] * o_win[rows])


def _attention(q, kc, vc, ks, vs, kw, vw, gates, *, tq=128, tk=512):
    B, H, S, dh = q.shape
    G = NSA_KV_HEADS
    n_seg = kc.shape[2]
    n_cmp = (S - CMP_LEN) // CMP_STRIDE + 1
    n_slc = S // SLC_LEN
    n_sel = min(N_SELECT, n_slc)
    assert n_seg == LANES and n_slc <= LANES and S % tk == 0 and tk % tq == 0 and S >= WINDOW + tq
    per_b = S // tq
    cs = np.arange(n_seg)[:, None] * CMP_STRIDE
    ss = np.arange(LANES)[None, :] * SLC_LEN
    ov = np.maximum(np.minimum(cs + CMP_LEN, ss + SLC_LEN) - np.maximum(cs, ss), 0).astype(np.float32) / CMP_LEN
    ov[n_cmp:, :] = 0.0
    ov[:, n_slc:] = 0.0
    kpos = np.arange(S).reshape(S // tk, 1, tk)
    ex = (kpos // SLC_LEN == np.arange(LANES)[None, :, None]).astype(np.float32)
    kvspec = pl.BlockSpec((1, 1, S, dh), lambda b, g, i: (b, g, 0, 0))
    cspec = pl.BlockSpec((1, 1, n_seg, dh), lambda b, g, i: (b, g, 0, 0))
    return pl.pallas_call(
        functools.partial(_attn_kernel, tq=tq, tk=tk, n_cmp=n_cmp, n_slc=n_slc, n_sel=n_sel),
        grid=(B, G, per_b),
        in_specs=[
            pl.BlockSpec((1, NSA_GROUP, tq, dh), lambda b, g, i: (b, g, i, 0)),
            cspec, cspec, kvspec, kvspec, kvspec, kvspec,
            pl.BlockSpec((tq, LANES), lambda b, g, i: (b * per_b + i, g)),
            pl.BlockSpec((n_seg, LANES), lambda b, g, i: (0, 0)),
            pl.BlockSpec((S // tk, LANES, tk), lambda b, g, i: (0, 0, 0)),
        ],
        out_specs=pl.BlockSpec((tq, NSA_GROUP * dh), lambda b, g, i: (b * per_b + i, g)),
        out_shape=jax.ShapeDtypeStruct((B * S, NSA_WIDTH), f32),
        compiler_params=_cparams("arbitrary", "arbitrary", "arbitrary"),
        name="nsa_attention",
    )(q, kc, vc, ks, vs, kw, vw, gates, jnp.asarray(ov, bf16), jnp.asarray(ex, bf16))


def _gelu(x):
    return 0.5 * x * (1.0 + lax.erf(x * (2.0 ** -0.5)))


def _rglru_kernel(xr_ref, xg_ref, cw_ref, cb_ref, wa_ref, ba_ref, wi_ref, bi_ref, lam_ref, o_ref,
                  ext_ref, a_ref, b_ref, hs_ref, hc_ref, *, ts):
    halo = 8

    @pl.when(pl.program_id(1) == 0)
    def _():
        ext_ref[0:halo] = jnp.zeros((halo, RNN_WIDTH), f32)
        hc_ref[...] = jnp.zeros_like(hc_ref)

    xr = xr_ref[...]
    ext_ref[halo:halo + ts] = xr
    u = cb_ref[...] + cw_ref[CONV_WIDTH - 1:CONV_WIDTH] * xr
    for k in range(1, CONV_WIDTH):
        u = u + cw_ref[CONV_WIDTH - 1 - k:CONV_WIDTH - k] * ext_ref[halo - k:halo - k + ts]
    ext_ref[0:halo] = xr[ts - halo:ts]

    ra, ri = [], []
    for n in range(RNN_BLOCKS):
        ub = u[:, n * RNN_BLOCK_DIM:(n + 1) * RNN_BLOCK_DIM].astype(bf16)
        ra.append(jnp.dot(ub, wa_ref[n], preferred_element_type=f32))
        ri.append(jnp.dot(ub, wi_ref[n], preferred_element_type=f32))
    r = jax.nn.sigmoid(jnp.concatenate(ra, axis=1) + ba_ref[...])
    ig = jax.nn.sigmoid(jnp.concatenate(ri, axis=1) + bi_ref[...])
    log_a = (-LRU_C) * r * jax.nn.softplus(-lam_ref[...])
    a_ref[...] = jnp.exp(log_a)
    b_ref[...] = jnp.sqrt(1.0 - jnp.exp(2.0 * log_a)) * ig * u

    sub = lax.broadcasted_iota(i32, (8, RNN_WIDTH), 0)

    def blk(j, h):
        r0 = pl.multiple_of(j * 8, 8)
        a8 = a_ref[pl.ds(r0, 8), :]
        b8 = b_ref[pl.ds(r0, 8), :]
        out = jnp.zeros((8, RNN_WIDTH), f32)
        for i in range(8):
            h = a8[i:i + 1] * h + b8[i:i + 1]
            out = jnp.where(sub == i, h, out)
        hs_ref[pl.ds(r0, 8), :] = out
        return h

    hc_ref[...] = lax.fori_loop(0, ts // 8, blk, hc_ref[...])
    o_ref[...] = _gelu(xg_ref[...]) * hs_ref[...]


def _rglru(z, conv_w, conv_b, wa, ba, wi, bi, lam, B, S, *, ts=512):
    W = RNN_WIDTH
    per_b = S // ts
    xr_idx = NSA_WIDTH // W
    vec = lambda a: a.reshape(1, W)
    full = lambda shape: pl.BlockSpec(shape, lambda b, i: (0,) * len(shape))
    return pl.pallas_call(
        functools.partial(_rglru_kernel, ts=ts),
        grid=(B, per_b),
        in_specs=[
            pl.BlockSpec((ts, W), lambda b, i: (b * per_b + i, xr_idx)),
            pl.BlockSpec((ts, W), lambda b, i: (b * per_b + i, xr_idx + 1)),
            full((CONV_WIDTH, W)), full((1, W)),
            full((RNN_BLOCKS, RNN_BLOCK_DIM, RNN_BLOCK_DIM)), full((1, W)),
            full((RNN_BLOCKS, RNN_BLOCK_DIM, RNN_BLOCK_DIM)), full((1, W)), full((1, W)),
        ],
        out_specs=pl.BlockSpec((ts, W), lambda b, i: (b * per_b + i, 0)),
        out_shape=jax.ShapeDtypeStruct((B * S, W), f32),
        scratch_shapes=[pltpu.VMEM((ts + 8, W), f32), pltpu.VMEM((ts, W), f32), pltpu.VMEM((ts, W), f32),
                        pltpu.VMEM((ts, W), f32), pltpu.VMEM((1, W), f32)],
        compiler_params=_cparams("arbitrary", "arbitrary"),
        name="rglru",
    )(z, z, conv_w, vec(conv_b), wa.astype(bf16), vec(ba), wi.astype(bf16), vec(bi), vec(lam))


def _outproj_kernel(oa_ref, or_ref, ga_ref, gr_ref, w_ref, x_ref, gt_ref, o_ref, yb_ref):
    @pl.when(pl.program_id(1) == 0)
    def _():
        wa = oa_ref.shape[1]
        yb_ref[:, 0:wa] = _rms(oa_ref[...], ga_ref[...]).astype(bf16)
        yb_ref[:, wa:] = _rms(or_ref[...], gr_ref[...]).astype(bf16)

    o_ref[...] = x_ref[...] + gt_ref[0] * jnp.dot(yb_ref[...], w_ref[...], preferred_element_type=f32)


def _outproj(o_attn, o_rnn, g_a, g_r, w, x, gt, seq, *, tm=512, tn=1024):
    T, D = x.shape
    B = gt.shape[0]
    wa, wr = o_attn.shape[1], o_rnn.shape[1]
    per_b = seq // tm
    return pl.pallas_call(
        _outproj_kernel,
        grid=(T // tm, D // tn),
        in_specs=[
            pl.BlockSpec((tm, wa), lambda i, j: (i, 0)),
            pl.BlockSpec((tm, wr), lambda i, j: (i, 0)),
            pl.BlockSpec((1, wa), lambda i, j: (0, 0)),
            pl.BlockSpec((1, wr), lambda i, j: (0, 0)),
            pl.BlockSpec((wa + wr, tn), lambda i, j: (0, j)),
            pl.BlockSpec((tm, tn), lambda i, j: (i, j)),
            pl.BlockSpec((1, 1, tn), lambda i, j: (i // per_b, 0, j)),
        ],
        out_specs=pl.BlockSpec((tm, tn), lambda i, j: (i, j)),
        out_shape=jax.ShapeDtypeStruct((T, D), f32),
        scratch_shapes=[pltpu.VMEM((tm, wa + wr), bf16)],
        compiler_params=_cparams("arbitrary", "arbitrary"),
        name="out_proj",
    )(o_attn, o_rnn, g_a.reshape(1, wa), g_r.reshape(1, wr), w, x, gt.reshape(B, 1, D))


def _top_rows(s, k):
    n = s.shape[0]
    io = lax.broadcasted_iota(i32, s.shape, 0)
    vals, idxs = [], []
    for _ in range(k):
        m = jnp.max(s, axis=0, keepdims=True)
        ix = jnp.min(jnp.where(s == m, io, n), axis=0, keepdims=True)
        vals.append(m)
        idxs.append(ix)
        s = jnp.where(io == ix, -jnp.inf, s)
    return vals, idxs


def _peer_topk_kernel(q_ref, keys_ref, e_ref, g_ref):
    K = PEER_TOPK
    tt = q_ref.shape[0]
    pairs = [(a, b) for a in range(K) for b in range(K) if (a + 1) * (b + 1) <= K]
    n_cand = len(pairs) + (-len(pairs) % 8)
    cio = lax.broadcasted_iota(i32, (n_cand, tt), 0)
    oio = lax.broadcasted_iota(i32, (PEER_HEADS * K, tt), 0)
    e_all = jnp.zeros((PEER_HEADS * K, tt), i32)
    g_all = jnp.zeros((PEER_HEADS * K, tt), f32)
    for h in range(PEER_HEADS):
        tops = []
        for p in range(2):
            c0 = (2 * h + p) * PEER_KEYS
            qhp = q_ref[:, c0:c0 + PEER_KEYS].astype(bf16)
            s = lax.dot_general(keys_ref[h, p], qhp, NT_DIMS, preferred_element_type=f32)
            tops.append(_top_rows(s, K))
        (v1, i1), (v2, i2) = tops
        cand = jnp.full((n_cand, tt), -jnp.inf, f32)
        cidx = jnp.zeros((n_cand, tt), i32)
        for r, (a, b) in enumerate(pairs):
            cand = jnp.where(cio == r, v1[a] + v2[b], cand)
            cidx = jnp.where(cio == r, i1[a] * PEER_KEYS + i2[b], cidx)
        vals, pos = _top_rows(cand, K)
        ex = [jnp.exp(v - vals[0]) for v in vals]
        den = functools.reduce(lambda x, y: x + y, ex)
        for r in range(K):
            e_r = jnp.sum(jnp.where(cio == pos[r], cidx, 0), axis=0, keepdims=True)
            e_all = jnp.where(oio == h * K + r, e_r, e_all)
            g_all = jnp.where(oio == h * K + r, ex[r] / den, g_all)
    e_ref[...] = e_all.T
    g_ref[...] = g_all.T


def _peer_topk(qp, keys, *, tt=256):
    T, N = qp.shape
    HK = PEER_HEADS * PEER_TOPK
    return pl.pallas_call(
        _peer_topk_kernel,
        grid=(T // tt,),
        in_specs=[pl.BlockSpec((tt, N), lambda i: (i, 0)),
                  pl.BlockSpec(keys.shape, lambda i: (0, 0, 0, 0))],
        out_specs=[pl.BlockSpec((tt, HK), lambda i: (i, 0))] * 2,
        out_shape=[jax.ShapeDtypeStruct((T, HK), i32), jax.ShapeDtypeStruct((T, HK), f32)],
        compiler_params=_cparams("arbitrary"),
        name="peer_topk",
    )(qp, keys.astype(bf16))


PEER_SLOTS = 4
PEER_AHEAD = 3


def _peer_ffn_kernel(idx_ref, g_ref, h_ref, x_ref, gt_ref, tab_ref, o_ref, buf_ref, sem_ref, *, tt):
    NE = idx_ref.shape[1]
    D = h_ref.shape[1]
    n_lt = D // LANES
    per_step = NE // (2 * n_lt)

    def issue(t, slot, ks):
        for k in ks:
            pltpu.make_async_copy(tab_ref.at[idx_ref[t, k]], buf_ref.at[slot, k], sem_ref.at[slot]).start(priority=k % 2)

    def wait(slot):
        pltpu.make_async_copy(tab_ref.at[pl.ds(0, NE)], buf_ref.at[slot], sem_ref.at[slot]).wait()

    diag = lax.broadcasted_iota(i32, (NE, LANES), 0) == lax.broadcasted_iota(i32, (NE, LANES), 1)
    ones8 = jnp.ones((8, LANES), bf16)
    ones = jnp.ones((LANES, LANES), bf16)

    def evaluate(t, slot, ahead):
        nslot = (slot + PEER_AHEAD) % PEER_SLOTS

        def step_done(i):
            if ahead:
                issue(t + PEER_AHEAD, nslot, range(i * per_step, (i + 1) * per_step))

        wait(slot)
        xrow = h_ref[pl.ds(t, 1), :]
        part = jnp.zeros((NE, LANES), f32)
        for c in range(n_lt):
            cols = slice(c * LANES, (c + 1) * LANES)
            part = part + pltpu.bitcast(buf_ref[slot, :, cols] << 16, f32) * xrow[:, cols]
            step_done(c)
        p_hi = part.astype(bf16)
        p_lo = (part - p_hi.astype(f32)).astype(bf16)
        act = (lax.dot_general(ones8, p_hi, NT_DIMS, preferred_element_type=f32)
               + lax.dot_general(ones8, p_lo, NT_DIMS, preferred_element_type=f32))[0:1]
        w_row = _gelu(act) * g_ref[pl.ds(t, 1), :]
        w_col = jnp.dot(jnp.where(diag, w_row, 0.0).astype(bf16), ones, preferred_element_type=f32)
        outs = []
        for c in range(n_lt):
            cols = slice(c * LANES, (c + 1) * LANES)
            up = pltpu.bitcast(buf_ref[slot, :, cols] & jnp.uint32(0xFFFF0000), f32)
            outs.append(jnp.sum(up * w_col, axis=0, keepdims=True))
            step_done(n_lt + c)
        y = jnp.concatenate(outs, axis=1)
        o_ref[pl.ds(t, 1), :] = x_ref[pl.ds(t, 1), :] + gt_ref[0] * y

    for t in range(PEER_AHEAD):
        issue(t, t, range(NE))

    def steady(j, carry):
        for u in range(PEER_SLOTS):
            evaluate(j * PEER_SLOTS + u, u, True)
        return carry

    lax.fori_loop(0, tt // PEER_SLOTS - 1, steady, 0)
    for u in range(PEER_SLOTS):
        evaluate(tt - PEER_SLOTS + u, u, u + PEER_AHEAD < PEER_SLOTS)


def _peer_ffn(eidx, gsm, h2, x1, gt2, table, seq, *, tt=128):
    T, D = h2.shape
    B = gt2.shape[0]
    NE = eidx.shape[1]
    assert 0 < PEER_AHEAD < PEER_SLOTS and tt % PEER_SLOTS == 0 and tt >= 2 * PEER_SLOTS and NE % (2 * D // LANES) == 0
    per_b = seq // tt
    return pl.pallas_call(
        functools.partial(_peer_ffn_kernel, tt=tt),
        grid=(T // tt,),
        in_specs=[
            pl.BlockSpec((tt, NE), lambda i: (i, 0), memory_space=pltpu.SMEM),
            pl.BlockSpec((tt, NE), lambda i: (i, 0)),
            pl.BlockSpec((tt, D), lambda i: (i, 0)),
            pl.BlockSpec((tt, D), lambda i: (i, 0)),
            pl.BlockSpec((1, 1, D), lambda i: (i // per_b, 0, 0)),
            pl.BlockSpec(memory_space=pl.ANY),
        ],
        out_specs=pl.BlockSpec((tt, D), lambda i: (i, 0)),
        out_shape=jax.ShapeDtypeStruct((T, D), f32),
        scratch_shapes=[pltpu.VMEM((PEER_SLOTS, NE, D), jnp.uint32), pltpu.SemaphoreType.DMA((PEER_SLOTS,))],
        compiler_params=_cparams("arbitrary"),
        name="peer_ffn",
    )(eidx, gsm, h2, x1, gt2.reshape(B, 1, D), table)


def _pack_kernel(d_ref, u_ref, o_ref):
    lo = pltpu.bitcast(d_ref[...].astype(bf16).astype(f32), jnp.uint32) >> 16
    hi = pltpu.bitcast(u_ref[...].astype(bf16).astype(f32), jnp.uint32)
    o_ref[...] = lo | hi


def _pack_experts(down, up, *, tb=512):
    E, D = down.shape
    spec = pl.BlockSpec((tb, D), lambda i: (i, 0))
    return pl.pallas_call(
        _pack_kernel,
        grid=(E // tb,),
        in_specs=[spec, spec],
        out_specs=spec,
        out_shape=jax.ShapeDtypeStruct((E, D), jnp.uint32),
        compiler_params=_cparams("arbitrary"),
        name="pack_experts",
    )(down, up)


def _rope_tables(S):
    half = HEAD_DIM // 2
    freqs = ROPE_THETA ** (-jnp.arange(half, dtype=f32) / half)
    ang = jnp.arange(S, dtype=f32)[:, None] * freqs[None, :]
    cos, sin = jnp.cos(ang), jnp.sin(ang)
    return jnp.concatenate([cos, cos], axis=1), jnp.concatenate([-sin, sin], axis=1)


def _layer(x, c, ada_w, ada_b, norm_mix_g, norm_ffn_g, w_in, w_out, q_norm_g, k_norm_g, cmp_pe_k, cmp_pe_v,
           cmp_w_k, cmp_w_v, gate_b, conv_w, conv_b, lru_wa, lru_ba, lru_wi, lru_bi, lru_lam, out_g_attn,
           out_g_rnn, peer_wq, peer_keys, peer_down, peer_up):
    B, S, D = x.shape
    T = B * S
    xf = x.reshape(T, D)
    mod = _mod(c, ada_w, ada_b)
    sh1, sc1, gt1, sh2, sc2, gt2 = [mod[:, k * D:(k + 1) * D] for k in range(6)]

    q0, g0 = NSA_WIDTH, NSA_WIDTH + 6 * KV_COLS
    r0 = g0 + GATE_COLS
    kvs = [w_in[:, q0 + k * KV_COLS:q0 + (k + 1) * KV_COLS] for k in range(6)]
    w_main = jnp.concatenate([w_in[:, :q0], w_in[:, r0:r0 + 2 * RNN_WIDTH]] + kvs, axis=1).astype(bf16)
    per_g = GATE_COLS // NSA_KV_HEADS
    wg = jnp.zeros((D, NSA_KV_HEADS * LANES), f32)
    gb = jnp.zeros((1, NSA_KV_HEADS * LANES), f32)
    for g in range(NSA_KV_HEADS):
        wg = wg.at[:, g * LANES:g * LANES + per_g].set(w_in[:, g0 + g * per_g:g0 + (g + 1) * per_g])
        gb = gb.at[0, g * LANES:g * LANES + per_g].set(gate_b[g * per_g:(g + 1) * per_g])
    z, gates = _norm_mm(xf, sc1, sh1, norm_mix_g, w_main, S, tm=512, tn=768, wg=wg.astype(bf16), gb=gb)

    cos, sin = _rope_tables(S)
    q, kc_r, vc_r, ks, vs, kw, vw = _prep(z, cos, sin, q_norm_g, k_norm_g, B, S)
    kc, vc = _compress(kc_r, vc_r, cmp_pe_k, cmp_pe_v, cmp_w_k, cmp_w_v, k_norm_g[0])
    o_attn = _attention(q, kc, vc, ks, vs, kw, vw, gates)
    o_rnn = _rglru(z, conv_w, conv_b, lru_wa, lru_ba, lru_wi, lru_bi, lru_lam, B, S)
    x1 = _outproj(o_attn, o_rnn, out_g_attn, out_g_rnn, w_out.astype(bf16), xf, gt1, S)

    qp, h2 = _norm_mm(x1, sc2, sh2, norm_ffn_g, peer_wq.astype(bf16), S, tm=512, tn=1024, emit_h=True)
    eidx, gsm = _peer_topk(qp, peer_keys)
    out = _peer_ffn(eidx, gsm, h2, x1, gt2, _pack_experts(peer_down, peer_up), S)
    return out.reshape(B, S, D)


def kernel(x, c, ada_w, ada_b, norm_mix_g, norm_ffn_g, w_in, w_out, q_norm_g, k_norm_g, cmp_pe_k, cmp_pe_v, cmp_w_k, cmp_w_v, gate_b, conv_w, conv_b, lru_wa, lru_ba, lru_wi, lru_bi, lru_lam, out_g_attn, out_g_rnn, peer_wq, peer_keys, peer_down, peer_up):
    params = (ada_w, ada_b, norm_mix_g, norm_ffn_g, w_in, w_out, q_norm_g, k_norm_g, cmp_pe_k, cmp_pe_v, cmp_w_k,
              cmp_w_v, gate_b, conv_w, conv_b, lru_wa, lru_ba, lru_wi, lru_bi, lru_lam, out_g_attn, out_g_rnn,
              peer_wq, peer_keys, peer_down, peer_up)
    for l in range(ada_w.shape[0]):
        x = _layer(x, c, *[p[l] for p in params])
    return x
```

```python
import functools

import numpy as np
import jax
import jax.numpy as jnp
from jax import lax
from jax.experimental import pallas as pl
from jax.experimental.pallas import tpu as pltpu

f32 = jnp.float32
bf16 = jnp.bfloat16
i32 = jnp.int32

HEAD_DIM = 128
NSA_HEADS = 8
NSA_KV_HEADS = 2
NSA_GROUP = NSA_HEADS // NSA_KV_HEADS
NSA_WIDTH = NSA_HEADS * HEAD_DIM
KV_COLS = NSA_KV_HEADS * HEAD_DIM
GATE_COLS = 3 * NSA_HEADS
CMP_LEN = 32
CMP_STRIDE = 16
SLC_LEN = 64
N_SELECT = 16
FORCED_SCORE = 1e4
WINDOW = 512
RNN_WIDTH = 1024
RNN_BLOCKS = 8
RNN_BLOCK_DIM = RNN_WIDTH // RNN_BLOCKS
CONV_WIDTH = 4
LRU_C = 8.0
PEER_HEADS = 8
PEER_KEYS = 128
PEER_TOPK = 16
ROPE_THETA = 10000.0
EPS = 1e-6
NEG_BIG = -1e30

LANES = 128
VMEM_LIMIT = 56 * 1024 * 1024

NT_DIMS = (((1,), (1,)), ((), ()))


def _cparams(*sem):
    return pltpu.CompilerParams(dimension_semantics=sem, vmem_limit_bytes=VMEM_LIMIT)


def _rms(x, g):
    return x * lax.rsqrt(jnp.mean(x * x, axis=-1, keepdims=True) + EPS) * g


def _mod_kernel(c_ref, w_ref, b_ref, o_ref):
    c = c_ref[...]
    a = c * jax.nn.sigmoid(c)
    o_ref[...] = jnp.dot(a, w_ref[...], preferred_element_type=f32) + b_ref[...]


def _mod(c, w, b, tn=1024):
    B, D = c.shape
    N = w.shape[1]
    return pl.pallas_call(
        _mod_kernel,
        grid=(N // tn,),
        in_specs=[
            pl.BlockSpec((B, D), lambda j: (0, 0)),
            pl.BlockSpec((D, tn), lambda j: (0, j)),
            pl.BlockSpec((1, tn), lambda j: (0, j)),
        ],
        out_specs=pl.BlockSpec((B, tn), lambda j: (0, j)),
        out_shape=jax.ShapeDtypeStruct((B, N), f32),
        compiler_params=_cparams("arbitrary"),
        name="mod",
    )(c, w, b.reshape(1, N))


def _norm_mm_kernel(x_ref, sc_ref, sh_ref, g_ref, w_ref, *rest, n_extra, emit_h):
    rest = list(rest)
    if n_extra:
        wg_ref, gb_ref = rest.pop(0), rest.pop(0)
    z_ref = rest.pop(0)
    if n_extra:
        gate_ref = rest.pop(0)
    if emit_h:
        h_ref = rest.pop(0)
    hb_ref = rest.pop(0)

    @pl.when(pl.program_id(1) == 0)
    def _():
        h = _rms(x_ref[...], g_ref[...]) * (1.0 + sc_ref[0]) + sh_ref[0]
        hb = h.astype(bf16)
        hb_ref[...] = hb
        if emit_h:
            h_ref[...] = h
        if n_extra:
            gl = jnp.dot(hb, wg_ref[...], preferred_element_type=f32)
            gate_ref[...] = jax.nn.sigmoid(gl + gb_ref[...])

    z_ref[...] = jnp.dot(hb_ref[...], w_ref[...], preferred_element_type=f32)


def _norm_mm(x, sc, sh, g, w, seq, *, tm, tn, wg=None, gb=None, emit_h=False):
    T, D = x.shape
    N = w.shape[1]
    B = sc.shape[0]
    assert seq % tm == 0 and N % tn == 0
    per_b = seq // tm
    n_extra = 0 if wg is None else wg.shape[1]
    in_specs = [
        pl.BlockSpec((tm, D), lambda i, j: (i, 0)),
        pl.BlockSpec((1, 1, D), lambda i, j: (i // per_b, 0, 0)),
        pl.BlockSpec((1, 1, D), lambda i, j: (i // per_b, 0, 0)),
        pl.BlockSpec((1, D), lambda i, j: (0, 0)),
        pl.BlockSpec((D, tn), lambda i, j: (0, j)),
    ]
    args = [x, sc.reshape(B, 1, D), sh.reshape(B, 1, D), g.reshape(1, D), w]
    out_specs = [pl.BlockSpec((tm, tn), lambda i, j: (i, j))]
    out_shape = [jax.ShapeDtypeStruct((T, N), f32)]
    if n_extra:
        in_specs += [pl.BlockSpec((D, n_extra), lambda i, j: (0, 0)), pl.BlockSpec((1, n_extra), lambda i, j: (0, 0))]
        args += [wg, gb]
        out_specs.append(pl.BlockSpec((tm, n_extra), lambda i, j: (i, 0)))
        out_shape.append(jax.ShapeDtypeStruct((T, n_extra), f32))
    if emit_h:
        out_specs.append(pl.BlockSpec((tm, D), lambda i, j: (i, 0)))
        out_shape.append(jax.ShapeDtypeStruct((T, D), f32))
    return pl.pallas_call(
        functools.partial(_norm_mm_kernel, n_extra=n_extra, emit_h=emit_h),
        grid=(T // tm, N // tn),
        in_specs=in_specs,
        out_specs=out_specs,
        out_shape=out_shape,
        scratch_shapes=[pltpu.VMEM((tm, D), bf16)],
        compiler_params=_cparams("arbitrary", "arbitrary"),
        name="norm_mm",
    )(*args)


def _prep_kernel(zq_ref, zc_ref, zs_ref, zw_ref, cos_ref, sin_ref, qg_ref, kg_ref,
                 q_ref, kc_ref, vc_ref, ks_ref, vs_ref, kw_ref, vw_ref):
    cos = cos_ref[...]
    sin = sin_ref[...]

    def rope(y):
        return y * cos + pltpu.roll(y, HEAD_DIM // 2, 1) * sin

    def head(ref, h):
        return ref[:, h * HEAD_DIM:(h + 1) * HEAD_DIM]

    qg = qg_ref[...]
    for h in range(NSA_HEADS):
        q_ref[0, h] = rope(_rms(head(zq_ref, h), qg)).astype(bf16)
    for g in range(NSA_KV_HEADS):
        kc_ref[0, g] = rope(head(zc_ref, g))
        vc_ref[0, g] = head(zc_ref, NSA_KV_HEADS + g)
        ks_ref[0, g] = rope(_rms(head(zs_ref, g), kg_ref[1:2])).astype(bf16)
        vs_ref[0, g] = head(zs_ref, NSA_KV_HEADS + g).astype(bf16)
        kw_ref[0, g] = rope(_rms(head(zw_ref, g), kg_ref[2:3])).astype(bf16)
        vw_ref[0, g] = head(zw_ref, NSA_KV_HEADS + g).astype(bf16)


def _prep(z, cos, sin, q_g, k_g, B, S, *, ts=256):
    per_b = S // ts
    G = NSA_KV_HEADS
    kv = 2 * KV_COLS
    base = (NSA_WIDTH + 2 * RNN_WIDTH) // kv

    def zspec(width, idx):
        return pl.BlockSpec((ts, width), lambda b, i: (b * per_b + i, idx))

    def hm(nh, dt):
        return pl.BlockSpec((1, nh, ts, HEAD_DIM), lambda b, i: (b, 0, i, 0)), jax.ShapeDtypeStruct((B, nh, S, HEAD_DIM), dt)

    outs = [hm(NSA_HEADS, bf16), hm(G, f32), hm(G, f32), hm(G, bf16), hm(G, bf16), hm(G, bf16), hm(G, bf16)]
    return pl.pallas_call(
        _prep_kernel,
        grid=(B, per_b),
        in_specs=[
            zspec(NSA_WIDTH, 0), zspec(kv, base), zspec(kv, base + 1), zspec(kv, base + 2),
            pl.BlockSpec((ts, HEAD_DIM), lambda b, i: (i, 0)),
            pl.BlockSpec((ts, HEAD_DIM), lambda b, i: (i, 0)),
            pl.BlockSpec((1, HEAD_DIM), lambda b, i: (0, 0)),
            pl.BlockSpec((3, HEAD_DIM), lambda b, i: (0, 0)),
        ],
        out_specs=[o[0] for o in outs],
        out_shape=[o[1] for o in outs],
        compiler_params=_cparams("arbitrary", "arbitrary"),
        name="prep",
    )(z, z, z, z, cos, sin, q_g.reshape(1, HEAD_DIM), k_g)


def _compress_kernel(kseg_ref, vseg_ref, pek_ref, pev_ref, wk_ref, wv_ref, kg_ref, kc_ref, vc_ref, *, n_cmp):
    n_seg = kseg_ref.shape[2]
    half = CMP_STRIDE * HEAD_DIM
    row = lax.broadcasted_iota(i32, (n_seg, HEAD_DIM), 0)

    def comp(seg_ref, pe_ref, w_ref):
        seg = seg_ref[0, 0]
        lo = jnp.dot((seg + pe_ref[0:1]).astype(bf16), w_ref[0:half], preferred_element_type=f32)
        hi = jnp.dot((seg + pe_ref[1:2]).astype(bf16), w_ref[half:2 * half], preferred_element_type=f32)
        return lo + pltpu.roll(hi, n_seg - 1, 0)

    kc = _rms(comp(kseg_ref, pek_ref, wk_ref), kg_ref[...])
    vc = comp(vseg_ref, pev_ref, wv_ref)
    kc_ref[0, 0] = jnp.where(row < n_cmp, kc, 0.0).astype(bf16)
    vc_ref[0, 0] = jnp.where(row < n_cmp, vc, 0.0).astype(bf16)


def _compress(kc_r, vc_r, pe_k, pe_v, w_k, w_v, k_g0):
    B, G, S, dh = kc_r.shape
    n_seg = S // CMP_STRIDE
    n_cmp = (S - CMP_LEN) // CMP_STRIDE + 1
    half = CMP_STRIDE * dh
    seg = lambda a: a.reshape(B, G, n_seg, half)
    segspec = pl.BlockSpec((1, 1, n_seg, half), lambda b, g: (b, g, 0, 0))
    full = lambda shape: pl.BlockSpec(shape, lambda b, g: (0,) * len(shape))
    ospec = pl.BlockSpec((1, 1, n_seg, dh), lambda b, g: (b, g, 0, 0))
    return pl.pallas_call(
        functools.partial(_compress_kernel, n_cmp=n_cmp),
        grid=(B, G),
        in_specs=[segspec, segspec, full((2, half)), full((2, half)), full((2 * half, dh)), full((2 * half, dh)),
                  full((1, dh))],
        out_specs=[ospec, ospec],
        out_shape=[jax.ShapeDtypeStruct((B, G, n_seg, dh), bf16)] * 2,
        compiler_params=_cparams("arbitrary", "arbitrary"),
        name="compress",
    )(seg(kc_r), seg(vc_r), pe_k.reshape(2, half), pe_v.reshape(2, half), w_k.astype(bf16), w_v.astype(bf16),
      k_g0.reshape(1, dh))


def _attn_kernel(q_ref, kc_ref, vc_ref, ks_ref, vs_ref, kw_ref, vw_ref, gate_ref, ov_ref, ex_ref, o_ref,
                 *, tq, tk, n_cmp, n_slc, n_sel):
    R = NSA_GROUP
    scale = HEAD_DIM ** -0.5
    t0 = pl.program_id(2) * tq
    q = q_ref[0].reshape(R * tq, HEAD_DIM)
    row_t = t0 + lax.broadcasted_iota(i32, (tq, 1), 0)
    row_t4 = t0 + (lax.broadcasted_iota(i32, (R * tq, 1), 0) & (tq - 1))
    per_head = lambda a: jnp.concatenate([a] * R, axis=0)

    ncl = kc_ref.shape[2]
    cl = lax.broadcasted_iota(i32, (1, ncl), 1)
    cmp_end = jnp.where(cl < n_cmp, cl * CMP_STRIDE + (CMP_LEN - 1), jnp.int32(2 ** 30))
    s = lax.dot_general(q, kc_ref[0, 0], NT_DIMS, preferred_element_type=f32) * scale
    s = jnp.where(cmp_end <= row_t4, s, -jnp.inf)
    m = jnp.max(s, axis=-1, keepdims=True)
    m = jnp.where(m == -jnp.inf, 0.0, m)
    p = jnp.exp(s - m)
    d = jnp.sum(p, axis=-1, keepdims=True)
    p = p / jnp.where(d > 0, d, 1.0)
    o_cmp = jnp.dot(p.astype(bf16), vc_ref[0, 0], preferred_element_type=f32)

    psum = p[0:tq]
    for r in range(1, R):
        psum = psum + p[r * tq:(r + 1) * tq]
    p_hi = psum.astype(bf16)
    p_lo = (psum - p_hi.astype(f32)).astype(bf16)
    imp = jnp.dot(p_hi, ov_ref[...], preferred_element_type=f32) + jnp.dot(p_lo, ov_ref[...], preferred_element_type=f32)
    jb = lax.broadcasted_iota(i32, (1, LANES), 1)
    cur = row_t // SLC_LEN
    imp = jnp.where(jb * SLC_LEN <= row_t, imp, -FORCED_SCORE)
    imp = jnp.where(jb == 0, FORCED_SCORE, jnp.where(jb == cur, FORCED_SCORE, jnp.where(jb == cur - 1, FORCED_SCORE, imp)))
    imp = jnp.where(jb < n_slc, imp, -jnp.inf)
    rank = jnp.zeros((tq, LANES), f32)
    for i in range(n_slc):
        col = imp[:, i:i + 1]
        later = jnp.where(jb > i, 1.0, 0.0)
        rank = rank + jnp.where(col > imp, 1.0, jnp.where(col == imp, later, 0.0))
    sel = jnp.where(rank < n_sel, jnp.where(jb < n_slc, 1.0, 0.0), 0.0).astype(bf16)

    def slc_chunk(c, carry):
        m_i, l_i, acc = carry
        k0 = pl.multiple_of(c * tk, tk)
        k = ks_ref[0, 0, pl.ds(k0, tk), :]
        v = vs_ref[0, 0, pl.ds(k0, tk), :]
        sc = lax.dot_general(q, k, NT_DIMS, preferred_element_type=f32) * scale
        kpos = k0 + lax.broadcasted_iota(i32, (1, tk), 1)
        picked = per_head(jnp.dot(sel, ex_ref[c], preferred_element_type=f32))
        sc = jnp.where(kpos <= row_t4, jnp.where(picked > 0.5, sc, NEG_BIG), NEG_BIG)
        m_n = jnp.maximum(m_i, jnp.max(sc, axis=-1, keepdims=True))
        alpha = jnp.exp(m_i - m_n)
        pc = jnp.exp(sc - m_n)
        l_n = alpha * l_i + jnp.sum(pc, axis=-1, keepdims=True)
        acc_n = alpha * acc + jnp.dot(pc.astype(bf16), v, preferred_element_type=f32)
        return m_n, l_n, acc_n

    n_chunks = (t0 + tq + tk - 1) // tk
    init = (jnp.full((R * tq, 1), NEG_BIG, f32), jnp.zeros((R * tq, 1), f32), jnp.zeros((R * tq, HEAD_DIM), f32))
    _, l_s, acc_s = lax.fori_loop(0, n_chunks, slc_chunk, init)
    o_slc = acc_s / l_s

    wk = WINDOW + tq
    w0 = pl.multiple_of(jnp.maximum(t0 - WINDOW, 0), tq)
    kpos = w0 + lax.broadcasted_iota(i32, (1, wk), 1)
    sw = lax.dot_general(q, kw_ref[0, 0, pl.ds(w0, wk), :], NT_DIMS, preferred_element_type=f32) * scale
    sw = jnp.where(kpos <= row_t4, jnp.where(kpos > row_t4 - WINDOW, sw, NEG_BIG), NEG_BIG)
    pw = jnp.exp(sw - jnp.max(sw, axis=-1, keepdims=True))
    o_win = jnp.dot(pw.astype(bf16), vw_ref[0, 0, pl.ds(w0, wk), :], preferred_element_type=f32)
    o_win = o_win / jnp.sum(pw, axis=-1, keepdims=True)

    gates = gate_ref[...]
    for r in range(R):
        rows = slice(r * tq, (r + 1) * tq)
        o_ref[:, r * HEAD_DIM:(r + 1) * HEAD_DIM] = (
            gates[:, 3 * r:3 * r + 1] * o_cmp[rows]
            + gates[:, 3 * r + 1:3 * r + 2] * o_slc[rows]
            + gates[:, 3 * r + 2:3 * r + 3] * o_win[rows])


def _attention(q, kc, vc, ks, vs, kw, vw, gates, *, tq=128, tk=512):
    B, H, S, dh = q.shape
    G = NSA_KV_HEADS
    n_seg = kc.shape[2]
    n_cmp = (S - CMP_LEN) // CMP_STRIDE + 1
    n_slc = S // SLC_LEN
    n_sel = min(N_SELECT, n_slc)
    assert n_seg == LANES and n_slc <= LANES and S % tk == 0 and tk % tq == 0 and S >= WINDOW + tq
    per_b = S // tq
    cs = np.arange(n_seg)[:, None] * CMP_STRIDE
    ss = np.arange(LANES)[None, :] * SLC_LEN
    ov = np.maximum(np.minimum(cs + CMP_LEN, ss + SLC_LEN) - np.maximum(cs, ss), 0).astype(np.float32) / CMP_LEN
    ov[n_cmp:, :] = 0.0
    ov[:, n_slc:] = 0.0
    kpos = np.arange(S).reshape(S // tk, 1, tk)
    ex = (kpos // SLC_LEN == np.arange(LANES)[None, :, None]).astype(np.float32)
    kvspec = pl.BlockSpec((1, 1, S, dh), lambda b, g, i: (b, g, 0, 0))
    cspec = pl.BlockSpec((1, 1, n_seg, dh), lambda b, g, i: (b, g, 0, 0))
    return pl.pallas_call(
        functools.partial(_attn_kernel, tq=tq, tk=tk, n_cmp=n_cmp, n_slc=n_slc, n_sel=n_sel),
        grid=(B, G, per_b),
        in_specs=[
            pl.BlockSpec((1, NSA_GROUP, tq, dh), lambda b, g, i: (b, g, i, 0)),
            cspec, cspec, kvspec, kvspec, kvspec, kvspec,
            pl.BlockSpec((tq, LANES), lambda b, g, i: (b * per_b + i, g)),
            pl.BlockSpec((n_seg, LANES), lambda b, g, i: (0, 0)),
            pl.BlockSpec((S // tk, LANES, tk), lambda b, g, i: (0, 0, 0)),
        ],
        out_specs=pl.BlockSpec((tq, NSA_GROUP * dh), lambda b, g, i: (b * per_b + i, g)),
        out_shape=jax.ShapeDtypeStruct((B * S, NSA_WIDTH), f32),
        compiler_params=_cparams("arbitrary", "arbitrary", "arbitrary"),
        name="nsa_attention",
    )(q, kc, vc, ks, vs, kw, vw, gates, jnp.asarray(ov, bf16), jnp.asarray(ex, bf16))


def _gelu(x):
    return 0.5 * x * (1.0 + lax.erf(x * (2.0 ** -0.5)))


def _rglru_kernel(xr_ref, xg_ref, cw_ref, cb_ref, wa_ref, ba_ref, wi_ref, bi_ref, lam_ref, o_ref,
                  ext_ref, a_ref, b_ref, hs_ref, hc_ref, *, ts):
    halo = 8

    @pl.when(pl.program_id(1) == 0)
    def _():
        ext_ref[0:halo] = jnp.zeros((halo, RNN_WIDTH), f32)
        hc_ref[...] = jnp.zeros_like(hc_ref)

    xr = xr_ref[...]
    ext_ref[halo:halo + ts] = xr
    u = cb_ref[...] + cw_ref[CONV_WIDTH - 1:CONV_WIDTH] * xr
    for k in range(1, CONV_WIDTH):
        u = u + cw_ref[CONV_WIDTH - 1 - k:CONV_WIDTH - k] * ext_ref[halo - k:halo - k + ts]
    ext_ref[0:halo] = xr[ts - halo:ts]

    ra, ri = [], []
    for n in range(RNN_BLOCKS):
        ub = u[:, n * RNN_BLOCK_DIM:(n + 1) * RNN_BLOCK_DIM].astype(bf16)
        ra.append(jnp.dot(ub, wa_ref[n], preferred_element_type=f32))
        ri.append(jnp.dot(ub, wi_ref[n], preferred_element_type=f32))
    r = jax.nn.sigmoid(jnp.concatenate(ra, axis=1) + ba_ref[...])
    ig = jax.nn.sigmoid(jnp.concatenate(ri, axis=1) + bi_ref[...])
    log_a = (-LRU_C) * r * jax.nn.softplus(-lam_ref[...])
    a_ref[...] = jnp.exp(log_a)
    b_ref[...] = jnp.sqrt(1.0 - jnp.exp(2.0 * log_a)) * ig * u

    sub = lax.broadcasted_iota(i32, (8, RNN_WIDTH), 0)

    def blk(j, h):
        r0 = pl.multiple_of(j * 8, 8)
        a8 = a_ref[pl.ds(r0, 8), :]
        b8 = b_ref[pl.ds(r0, 8), :]
        out = jnp.zeros((8, RNN_WIDTH), f32)
        for i in range(8):
            h = a8[i:i + 1] * h + b8[i:i + 1]
            out = jnp.where(sub == i, h, out)
        hs_ref[pl.ds(r0, 8), :] = out
        return h

    hc_ref[...] = lax.fori_loop(0, ts // 8, blk, hc_ref[...])
    o_ref[...] = _gelu(xg_ref[...]) * hs_ref[...]


def _rglru(z, conv_w, conv_b, wa, ba, wi, bi, lam, B, S, *, ts=512):
    W = RNN_WIDTH
    per_b = S // ts
    xr_idx = NSA_WIDTH // W
    vec = lambda a: a.reshape(1, W)
    full = lambda shape: pl.BlockSpec(shape, lambda b, i: (0,) * len(shape))
    return pl.pallas_call(
        functools.partial(_rglru_kernel, ts=ts),
        grid=(B, per_b),
        in_specs=[
            pl.BlockSpec((ts, W), lambda b, i: (b * per_b + i, xr_idx)),
            pl.BlockSpec((ts, W), lambda b, i: (b * per_b + i, xr_idx + 1)),
            full((CONV_WIDTH, W)), full((1, W)),
            full((RNN_BLOCKS, RNN_BLOCK_DIM, RNN_BLOCK_DIM)), full((1, W)),
            full((RNN_BLOCKS, RNN_BLOCK_DIM, RNN_BLOCK_DIM)), full((1, W)), full((1, W)),
        ],
        out_specs=pl.BlockSpec((ts, W), lambda b, i: (b * per_b + i, 0)),
        out_shape=jax.ShapeDtypeStruct((B * S, W), f32),
        scratch_shapes=[pltpu.VMEM((ts + 8, W), f32), pltpu.VMEM((ts, W), f32), pltpu.VMEM((ts, W), f32),
                        pltpu.VMEM((ts, W), f32), pltpu.VMEM((1, W), f32)],
        compiler_params=_cparams("arbitrary", "arbitrary"),
        name="rglru",
    )(z, z, conv_w, vec(conv_b), wa.astype(bf16), vec(ba), wi.astype(bf16), vec(bi), vec(lam))


def _outproj_kernel(oa_ref, or_ref, ga_ref, gr_ref, w_ref, x_ref, gt_ref, o_ref, yb_ref):
    @pl.when(pl.program_id(1) == 0)
    def _():
        wa = oa_ref.shape[1]
        yb_ref[:, 0:wa] = _rms(oa_ref[...], ga_ref[...]).astype(bf16)
        yb_ref[:, wa:] = _rms(or_ref[...], gr_ref[...]).astype(bf16)

    o_ref[...] = x_ref[...] + gt_ref[0] * jnp.dot(yb_ref[...], w_ref[...], preferred_element_type=f32)


def _outproj(o_attn, o_rnn, g_a, g_r, w, x, gt, seq, *, tm=512, tn=1024):
    T, D = x.shape
    B = gt.shape[0]
    wa, wr = o_attn.shape[1], o_rnn.shape[1]
    per_b = seq // tm
    return pl.pallas_call(
        _outproj_kernel,
        grid=(T // tm, D // tn),
        in_specs=[
            pl.BlockSpec((tm, wa), lambda i, j: (i, 0)),
            pl.BlockSpec((tm, wr), lambda i, j: (i, 0)),
            pl.BlockSpec((1, wa), lambda i, j: (0, 0)),
            pl.BlockSpec((1, wr), lambda i, j: (0, 0)),
            pl.BlockSpec((wa + wr, tn), lambda i, j: (0, j)),
            pl.BlockSpec((tm, tn), lambda i, j: (i, j)),
            pl.BlockSpec((1, 1, tn), lambda i, j: (i // per_b, 0, j)),
        ],
        out_specs=pl.BlockSpec((tm, tn), lambda i, j: (i, j)),
        out_shape=jax.ShapeDtypeStruct((T, D), f32),
        scratch_shapes=[pltpu.VMEM((tm, wa + wr), bf16)],
        compiler_params=_cparams("arbitrary", "arbitrary"),
        name="out_proj",
    )(o_attn, o_rnn, g_a.reshape(1, wa), g_r.reshape(1, wr), w, x, gt.reshape(B, 1, D))


def _top_rows(s, k):
    n = s.shape[0]
    io = lax.broadcasted_iota(i32, s.shape, 0)
    vals, idxs = [], []
    for _ in range(k):
        m = jnp.max(s, axis=0, keepdims=True)
        ix = jnp.min(jnp.where(s == m, io, n), axis=0, keepdims=True)
        vals.append(m)
        idxs.append(ix)
        s = jnp.where(io == ix, -jnp.inf, s)
    return vals, idxs


def _peer_topk_kernel(q_ref, keys_ref, e_ref, g_ref):
    K = PEER_TOPK
    tt = q_ref.shape[0]
    pairs = [(a, b) for a in range(K) for b in range(K) if (a + 1) * (b + 1) <= K]
    n_cand = len(pairs) + (-len(pairs) % 8)
    cio = lax.broadcasted_iota(i32, (n_cand, tt), 0)
    oio = lax.broadcasted_iota(i32, (PEER_HEADS * K, tt), 0)
    e_all = jnp.zeros((PEER_HEADS * K, tt), i32)
    g_all = jnp.zeros((PEER_HEADS * K, tt), f32)
    for h in range(PEER_HEADS):
        tops = []
        for p in range(2):
            c0 = (2 * h + p) * PEER_KEYS
            qhp = q_ref[:, c0:c0 + PEER_KEYS].astype(bf16)
            s = lax.dot_general(keys_ref[h, p], qhp, NT_DIMS, preferred_element_type=f32)
            tops.append(_top_rows(s, K))
        (v1, i1), (v2, i2) = tops
        cand = jnp.full((n_cand, tt), -jnp.inf, f32)
        cidx = jnp.zeros((n_cand, tt), i32)
        for r, (a, b) in enumerate(pairs):
            cand = jnp.where(cio == r, v1[a] + v2[b], cand)
            cidx = jnp.where(cio == r, i1[a] * PEER_KEYS + i2[b], cidx)
        vals, pos = _top_rows(cand, K)
        ex = [jnp.exp(v - vals[0]) for v in vals]
        den = functools.reduce(lambda x, y: x + y, ex)
        for r in range(K):
            e_r = jnp.sum(jnp.where(cio == pos[r], cidx, 0), axis=0, keepdims=True)
            e_all = jnp.where(oio == h * K + r, e_r, e_all)
            g_all = jnp.where(oio == h * K + r, ex[r] / den, g_all)
    e_ref[...] = e_all.T
    g_ref[...] = g_all.T


def _peer_topk(qp, keys, *, tt=256):
    T, N = qp.shape
    HK = PEER_HEADS * PEER_TOPK
    return pl.pallas_call(
        _peer_topk_kernel,
        grid=(T // tt,),
        in_specs=[pl.BlockSpec((tt, N), lambda i: (i, 0)),
                  pl.BlockSpec(keys.shape, lambda i: (0, 0, 0, 0))],
        out_specs=[pl.BlockSpec((tt, HK), lambda i: (i, 0))] * 2,
        out_shape=[jax.ShapeDtypeStruct((T, HK), i32), jax.ShapeDtypeStruct((T, HK), f32)],
        compiler_params=_cparams("arbitrary"),
        name="peer_topk",
    )(qp, keys.astype(bf16))


PEER_GROUP = 2
PEER_AHEAD = 3
PEER_SLOTS = PEER_GROUP * (PEER_AHEAD + 1)


def _peer_ffn_kernel(idx_ref, g_ref, h_ref, x_ref, gt_ref, tab_ref, o_ref, *scratch, tt):
    bufs, sem_ref = scratch[:PEER_SLOTS], scratch[PEER_SLOTS]
    NE = idx_ref.shape[1]
    D = h_ref.shape[1]
    n_lt = D // LANES
    per_step = NE // (2 * n_lt)
    ahead_tokens = PEER_GROUP * PEER_AHEAD

    def issue(t, slot, ks):
        for k in ks:
            pltpu.make_async_copy(tab_ref.at[idx_ref[t, k], 0], bufs[slot].at[k], sem_ref.at[slot]).start(priority=k % 2)

    def wait(slot):
        pltpu.make_async_copy(tab_ref.at[pl.ds(0, NE), 0], bufs[slot], sem_ref.at[slot]).wait()

    diag = lax.broadcasted_iota(i32, (NE, LANES), 0) == lax.broadcasted_iota(i32, (NE, LANES), 1)
    ones8 = jnp.ones((8, LANES), bf16)
    ones = jnp.ones((LANES, LANES), bf16)

    def evaluate(t, slot, ahead):
        nslot = (slot + ahead_tokens) % PEER_SLOTS

        def step_done(i):
            if ahead:
                issue(t + ahead_tokens, nslot, range(i * per_step, (i + 1) * per_step))

        xrow = h_ref[pl.ds(t, 1), :]
        part = jnp.zeros((NE, LANES), f32)
        for c in range(n_lt):
            cols = slice(c * LANES, (c + 1) * LANES)
            part = part + pltpu.bitcast(bufs[slot][:, cols] << 16, f32) * xrow[:, cols]
            step_done(c)
        g_col = jnp.sum(jnp.where(diag, g_ref[pl.ds(t, 1), :], 0.0), axis=1, keepdims=True)
        act = jnp.sum(part, axis=1, keepdims=True)
        w_col = jnp.broadcast_to(_gelu(act) * g_col, (NE, LANES))
        outs = []
        for c in range(n_lt):
            cols = slice(c * LANES, (c + 1) * LANES)
            up = pltpu.bitcast(bufs[slot][:, cols] & jnp.uint32(0xFFFF0000), f32)
            outs.append(jnp.sum(up * w_col, axis=0, keepdims=True))
            step_done(n_lt + c)
        y = jnp.concatenate(outs, axis=1)
        o_ref[pl.ds(t, 1), :] = x_ref[pl.ds(t, 1), :] + gt_ref[0] * y

    def group(t0, s0, ahead):
        for u in range(PEER_GROUP):
            wait(s0 + u)
        for u in range(PEER_GROUP):
            evaluate(t0 + u, s0 + u, ahead)

    for t in range(ahead_tokens):
        issue(t, t, range(NE))

    def steady(j, carry):
        for s0 in range(0, PEER_SLOTS, PEER_GROUP):
            group(j * PEER_SLOTS + s0, s0, True)
        return carry

    lax.fori_loop(0, tt // PEER_SLOTS - 1, steady, 0)
    for s0 in range(0, PEER_SLOTS, PEER_GROUP):
        group(tt - PEER_SLOTS + s0, s0, s0 + ahead_tokens < PEER_SLOTS)


def _peer_ffn(eidx, gsm, h2, x1, gt2, table, seq, *, tt=128):
    T, D = h2.shape
    B = gt2.shape[0]
    NE = eidx.shape[1]
    assert tt % PEER_SLOTS == 0 and tt >= 2 * PEER_SLOTS and NE % (2 * D // LANES) == 0
    per_b = seq // tt
    return pl.pallas_call(
        functools.partial(_peer_ffn_kernel, tt=tt),
        grid=(T // tt,),
        in_specs=[
            pl.BlockSpec((tt, NE), lambda i: (i, 0), memory_space=pltpu.SMEM),
            pl.BlockSpec((tt, NE), lambda i: (i, 0)),
            pl.BlockSpec((tt, D), lambda i: (i, 0)),
            pl.BlockSpec((tt, D), lambda i: (i, 0)),
            pl.BlockSpec((1, 1, D), lambda i: (i // per_b, 0, 0)),
            pl.BlockSpec(memory_space=pl.ANY),
        ],
        out_specs=pl.BlockSpec((tt, D), lambda i: (i, 0)),
        out_shape=jax.ShapeDtypeStruct((T, D), f32),
        scratch_shapes=[pltpu.VMEM((NE, D), jnp.uint32)] * PEER_SLOTS + [pltpu.SemaphoreType.DMA((PEER_SLOTS,))],
        compiler_params=_cparams("arbitrary"),
        name="peer_ffn",
    )(eidx, gsm, h2, x1, gt2.reshape(B, 1, D), table)


SUBLANES = 8


def _pack_kernel(d_ref, u_ref, o_ref):
    lo = pltpu.bitcast(d_ref[...].astype(bf16).astype(f32), jnp.uint32) >> 16
    hi = pltpu.bitcast(u_ref[...].astype(bf16).astype(f32), jnp.uint32)
    o_ref[...] = jnp.zeros(o_ref.shape, jnp.uint32)
    o_ref[:, 0, :] = lo | hi


def _pack_experts(down, up, *, tb=64):
    E, D = down.shape
    spec = pl.BlockSpec((tb, D), lambda i: (i, 0))
    return pl.pallas_call(
        _pack_kernel,
        grid=(E // tb,),
        in_specs=[spec, spec],
        out_specs=pl.BlockSpec((tb, SUBLANES, D), lambda i: (i, 0, 0)),
        out_shape=jax.ShapeDtypeStruct((E, SUBLANES, D), jnp.uint32),
        compiler_params=_cparams("arbitrary"),
        name="pack_experts",
    )(down, up)


def _rope_tables(S):
    half = HEAD_DIM // 2
    freqs = ROPE_THETA ** (-jnp.arange(half, dtype=f32) / half)
    ang = jnp.arange(S, dtype=f32)[:, None] * freqs[None, :]
    cos, sin = jnp.cos(ang), jnp.sin(ang)
    return jnp.concatenate([cos, cos], axis=1), jnp.concatenate([-sin, sin], axis=1)


def _layer(x, c, ada_w, ada_b, norm_mix_g, norm_ffn_g, w_in, w_out, q_norm_g, k_norm_g, cmp_pe_k, cmp_pe_v,
           cmp_w_k, cmp_w_v, gate_b, conv_w, conv_b, lru_wa, lru_ba, lru_wi, lru_bi, lru_lam, out_g_attn,
           out_g_rnn, peer_wq, peer_keys, peer_down, peer_up):
    B, S, D = x.shape
    T = B * S
    xf = x.reshape(T, D)
    mod = _mod(c, ada_w, ada_b)
    sh1, sc1, gt1, sh2, sc2, gt2 = [mod[:, k * D:(k + 1) * D] for k in range(6)]

    q0, g0 = NSA_WIDTH, NSA_WIDTH + 6 * KV_COLS
    r0 = g0 + GATE_COLS
    kvs = [w_in[:, q0 + k * KV_COLS:q0 + (k + 1) * KV_COLS] for k in range(6)]
    w_main = jnp.concatenate([w_in[:, :q0], w_in[:, r0:r0 + 2 * RNN_WIDTH]] + kvs, axis=1).astype(bf16)
    per_g = GATE_COLS // NSA_KV_HEADS
    wg = jnp.zeros((D, NSA_KV_HEADS * LANES), f32)
    gb = jnp.zeros((1, NSA_KV_HEADS * LANES), f32)
    for g in range(NSA_KV_HEADS):
        wg = wg.at[:, g * LANES:g * LANES + per_g].set(w_in[:, g0 + g * per_g:g0 + (g + 1) * per_g])
        gb = gb.at[0, g * LANES:g * LANES + per_g].set(gate_b[g * per_g:(g + 1) * per_g])
    z, gates = _norm_mm(xf, sc1, sh1, norm_mix_g, w_main, S, tm=512, tn=768, wg=wg.astype(bf16), gb=gb)

    cos, sin = _rope_tables(S)
    q, kc_r, vc_r, ks, vs, kw, vw = _prep(z, cos, sin, q_norm_g, k_norm_g, B, S)
    kc, vc = _compress(kc_r, vc_r, cmp_pe_k, cmp_pe_v, cmp_w_k, cmp_w_v, k_norm_g[0])
    o_attn = _attention(q, kc, vc, ks, vs, kw, vw, gates)
    o_rnn = _rglru(z, conv_w, conv_b, lru_wa, lru_ba, lru_wi, lru_bi, lru_lam, B, S)
    x1 = _outproj(o_attn, o_rnn, out_g_attn, out_g_rnn, w_out.astype(bf16), xf, gt1, S)

    qp, h2 = _norm_mm(x1, sc2, sh2, norm_ffn_g, peer_wq.astype(bf16), S, tm=512, tn=1024, emit_h=True)
    eidx, gsm = _peer_topk(qp, peer_keys)
    out = _peer_ffn(eidx, gsm, h2, x1, gt2, _pack_experts(peer_down, peer_up), S)
    return out.reshape(B, S, D)


def kernel(x, c, ada_w, ada_b, norm_mix_g, norm_ffn_g, w_in, w_out, q_norm_g, k_norm_g, cmp_pe_k, cmp_pe_v, cmp_w_k, cmp_w_v, gate_b, conv_w, conv_b, lru_wa, lru_ba, lru_wi, lru_bi, lru_lam, out_g_attn, out_g_rnn, peer_wq, peer_keys, peer_down, peer_up):
    params = (ada_w, ada_b, norm_mix_g, norm_ffn_g, w_in, w_out, q_norm_g, k_norm_g, cmp_pe_k, cmp_pe_v, cmp_w_k,
              cmp_w_v, gate_b, conv_w, conv_b, lru_wa, lru_ba, lru_wi, lru_bi, lru_lam, out_g_attn, out_g_rnn,
              peer_wq, peer_keys, peer_down, peer_up)
    for l in range(ada_w.shape[0]):
        x = _layer(x, c, *[p[l] for p in params])
    return x
```

```python
import functools

import numpy as np
import jax
import jax.numpy as jnp
from jax import lax
from jax.experimental import pallas as pl
from jax.experimental.pallas import tpu as pltpu

f32 = jnp.float32
bf16 = jnp.bfloat16
i32 = jnp.int32

HEAD_DIM = 128
NSA_HEADS = 8
NSA_KV_HEADS = 2
NSA_GROUP = NSA_HEADS // NSA_KV_HEADS
NSA_WIDTH = NSA_HEADS * HEAD_DIM
KV_COLS = NSA_KV_HEADS * HEAD_DIM
GATE_COLS = 3 * NSA_HEADS
CMP_LEN = 32
CMP_STRIDE = 16
SLC_LEN = 64
N_SELECT = 16
FORCED_SCORE = 1e4
WINDOW = 512
RNN_WIDTH = 1024
RNN_BLOCKS = 8
RNN_BLOCK_DIM = RNN_WIDTH // RNN_BLOCKS
CONV_WIDTH = 4
LRU_C = 8.0
PEER_HEADS = 8
PEER_KEYS = 128
PEER_TOPK = 16
ROPE_THETA = 10000.0
EPS = 1e-6
NEG_BIG = -1e30

LANES = 128
SUBLANES = 8
VMEM_LIMIT = 56 * 1024 * 1024

NT_DIMS = (((1,), (1,)), ((), ()))


def _cparams(*sem):
    return pltpu.CompilerParams(dimension_semantics=sem, vmem_limit_bytes=VMEM_LIMIT)


def _rms(x, g):
    return x * lax.rsqrt(jnp.mean(x * x, axis=-1, keepdims=True) + EPS) * g


def _mod_kernel(c_ref, w_ref, b_ref, o_ref):
    c = c_ref[...]
    a = c * jax.nn.sigmoid(c)
    o_ref[...] = jnp.dot(a, w_ref[...], preferred_element_type=f32) + b_ref[...]


def _mod(c, w, b, tn=1024):
    B, D = c.shape
    N = w.shape[1]
    return pl.pallas_call(
        _mod_kernel,
        grid=(N // tn,),
        in_specs=[
            pl.BlockSpec((B, D), lambda j: (0, 0)),
            pl.BlockSpec((D, tn), lambda j: (0, j)),
            pl.BlockSpec((1, tn), lambda j: (0, j)),
        ],
        out_specs=pl.BlockSpec((B, tn), lambda j: (0, j)),
        out_shape=jax.ShapeDtypeStruct((B, N), f32),
        compiler_params=_cparams("arbitrary"),
        name="mod",
    )(c, w, b.reshape(1, N))


def _norm_mm_kernel(x_ref, sc_ref, sh_ref, g_ref, w_ref, *rest, n_extra, emit_h):
    rest = list(rest)
    if n_extra:
        wg_ref, gb_ref = rest.pop(0), rest.pop(0)
    z_ref = rest.pop(0)
    if n_extra:
        gate_ref = rest.pop(0)
    if emit_h:
        h_ref = rest.pop(0)
    hb_ref = rest.pop(0)

    @pl.when(pl.program_id(1) == 0)
    def _():
        h = _rms(x_ref[...], g_ref[...]) * (1.0 + sc_ref[0]) + sh_ref[0]
        hb = h.astype(bf16)
        hb_ref[...] = hb
        if emit_h:
            h_ref[...] = h
        if n_extra:
            gl = jnp.dot(hb, wg_ref[...], preferred_element_type=f32)
            gate_ref[...] = jax.nn.sigmoid(gl + gb_ref[...])

    z_ref[...] = jnp.dot(hb_ref[...], w_ref[...], preferred_element_type=f32)


def _norm_mm(x, sc, sh, g, w, seq, *, tm, tn, wg=None, gb=None, emit_h=False):
    T, D = x.shape
    N = w.shape[1]
    B = sc.shape[0]
    assert seq % tm == 0 and N % tn == 0
    per_b = seq // tm
    n_extra = 0 if wg is None else wg.shape[1]
    in_specs = [
        pl.BlockSpec((tm, D), lambda i, j: (i, 0)),
        pl.BlockSpec((1, 1, D), lambda i, j: (i // per_b, 0, 0)),
        pl.BlockSpec((1, 1, D), lambda i, j: (i // per_b, 0, 0)),
        pl.BlockSpec((1, D), lambda i, j: (0, 0)),
        pl.BlockSpec((D, tn), lambda i, j: (0, j)),
    ]
    args = [x, sc.reshape(B, 1, D), sh.reshape(B, 1, D), g.reshape(1, D), w]
    out_specs = [pl.BlockSpec((tm, tn), lambda i, j: (i, j))]
    out_shape = [jax.ShapeDtypeStruct((T, N), f32)]
    if n_extra:
        in_specs += [pl.BlockSpec((D, n_extra), lambda i, j: (0, 0)), pl.BlockSpec((1, n_extra), lambda i, j: (0, 0))]
        args += [wg, gb]
        out_specs.append(pl.BlockSpec((tm, n_extra), lambda i, j: (i, 0)))
        out_shape.append(jax.ShapeDtypeStruct((T, n_extra), f32))
    if emit_h:
        out_specs.append(pl.BlockSpec((tm, D), lambda i, j: (i, 0)))
        out_shape.append(jax.ShapeDtypeStruct((T, D), f32))
    return pl.pallas_call(
        functools.partial(_norm_mm_kernel, n_extra=n_extra, emit_h=emit_h),
        grid=(T // tm, N // tn),
        in_specs=in_specs,
        out_specs=out_specs,
        out_shape=out_shape,
        scratch_shapes=[pltpu.VMEM((tm, D), bf16)],
        compiler_params=_cparams("arbitrary", "arbitrary"),
        name="norm_mm",
    )(*args)


def _prep_kernel(zq_ref, zc_ref, zs_ref, zw_ref, cos_ref, sin_ref, qg_ref, kg_ref,
                 q_ref, kc_ref, vc_ref, ks_ref, vs_ref, kw_ref, vw_ref):
    cos = cos_ref[...]
    sin = sin_ref[...]

    def rope(y):
        return y * cos + pltpu.roll(y, HEAD_DIM // 2, 1) * sin

    def head(ref, h):
        return ref[:, h * HEAD_DIM:(h + 1) * HEAD_DIM]

    qg = qg_ref[...]
    for h in range(NSA_HEADS):
        q_ref[0, h] = rope(_rms(head(zq_ref, h), qg)).astype(bf16)
    for g in range(NSA_KV_HEADS):
        kc_ref[0, g] = rope(head(zc_ref, g))
        vc_ref[0, g] = head(zc_ref, NSA_KV_HEADS + g)
        ks_ref[0, g] = rope(_rms(head(zs_ref, g), kg_ref[1:2])).astype(bf16)
        vs_ref[0, g] = head(zs_ref, NSA_KV_HEADS + g).astype(bf16)
        kw_ref[0, g] = rope(_rms(head(zw_ref, g), kg_ref[2:3])).astype(bf16)
        vw_ref[0, g] = head(zw_ref, NSA_KV_HEADS + g).astype(bf16)


def _prep(z, cos, sin, q_g, k_g, B, S, *, ts=256):
    per_b = S // ts
    G = NSA_KV_HEADS
    kv = 2 * KV_COLS
    base = (NSA_WIDTH + 2 * RNN_WIDTH) // kv

    def zspec(width, idx):
        return pl.BlockSpec((ts, width), lambda b, i: (b * per_b + i, idx))

    def hm(nh, dt):
        return pl.BlockSpec((1, nh, ts, HEAD_DIM), lambda b, i: (b, 0, i, 0)), jax.ShapeDtypeStruct((B, nh, S, HEAD_DIM), dt)

    outs = [hm(NSA_HEADS, bf16), hm(G, f32), hm(G, f32), hm(G, bf16), hm(G, bf16), hm(G, bf16), hm(G, bf16)]
    return pl.pallas_call(
        _prep_kernel,
        grid=(B, per_b),
        in_specs=[
            zspec(NSA_WIDTH, 0), zspec(kv, base), zspec(kv, base + 1), zspec(kv, base + 2),
            pl.BlockSpec((ts, HEAD_DIM), lambda b, i: (i, 0)),
            pl.BlockSpec((ts, HEAD_DIM), lambda b, i: (i, 0)),
            pl.BlockSpec((1, HEAD_DIM), lambda b, i: (0, 0)),
            pl.BlockSpec((3, HEAD_DIM), lambda b, i: (0, 0)),
        ],
        out_specs=[o[0] for o in outs],
        out_shape=[o[1] for o in outs],
        compiler_params=_cparams("arbitrary", "arbitrary"),
        name="prep",
    )(z, z, z, z, cos, sin, q_g.reshape(1, HEAD_DIM), k_g)


def _compress_kernel(kseg_ref, vseg_ref, pek_ref, pev_ref, wk_ref, wv_ref, kg_ref, kc_ref, vc_ref, *, n_cmp):
    n_seg = kseg_ref.shape[2]
    half = CMP_STRIDE * HEAD_DIM
    row = lax.broadcasted_iota(i32, (n_seg, HEAD_DIM), 0)

    def comp(seg_ref, pe_ref, w_ref):
        seg = seg_ref[0, 0]
        lo = jnp.dot((seg + pe_ref[0:1]).astype(bf16), w_ref[0:half], preferred_element_type=f32)
        hi = jnp.dot((seg + pe_ref[1:2]).astype(bf16), w_ref[half:2 * half], preferred_element_type=f32)
        return lo + pltpu.roll(hi, n_seg - 1, 0)

    kc = _rms(comp(kseg_ref, pek_ref, wk_ref), kg_ref[...])
    vc = comp(vseg_ref, pev_ref, wv_ref)
    kc_ref[0, 0] = jnp.where(row < n_cmp, kc, 0.0).astype(bf16)
    vc_ref[0, 0] = jnp.where(row < n_cmp, vc, 0.0).astype(bf16)


def _compress(kc_r, vc_r, pe_k, pe_v, w_k, w_v, k_g0):
    B, G, S, dh = kc_r.shape
    n_seg = S // CMP_STRIDE
    n_cmp = (S - CMP_LEN) // CMP_STRIDE + 1
    half = CMP_STRIDE * dh
    seg = lambda a: a.reshape(B, G, n_seg, half)
    segspec = pl.BlockSpec((1, 1, n_seg, half), lambda b, g: (b, g, 0, 0))
    full = lambda shape: pl.BlockSpec(shape, lambda b, g: (0,) * len(shape))
    ospec = pl.BlockSpec((1, 1, n_seg, dh), lambda b, g: (b, g, 0, 0))
    return pl.pallas_call(
        functools.partial(_compress_kernel, n_cmp=n_cmp),
        grid=(B, G),
        in_specs=[segspec, segspec, full((2, half)), full((2, half)), full((2 * half, dh)), full((2 * half, dh)),
                  full((1, dh))],
        out_specs=[ospec, ospec],
        out_shape=[jax.ShapeDtypeStruct((B, G, n_seg, dh), bf16)] * 2,
        compiler_params=_cparams("arbitrary", "arbitrary"),
        name="compress",
    )(seg(kc_r), seg(vc_r), pe_k.reshape(2, half), pe_v.reshape(2, half), w_k.astype(bf16), w_v.astype(bf16),
      k_g0.reshape(1, dh))


def _attn_kernel(q_ref, kc_ref, vc_ref, ks_ref, vs_ref, kw_ref, vw_ref, gate_ref, ov_ref, ex_ref, o_ref,
                 *, tq, tk, n_cmp, n_slc, n_sel):
    R = NSA_GROUP
    scale = HEAD_DIM ** -0.5
    t0 = pl.program_id(2) * tq
    q = q_ref[0].reshape(R * tq, HEAD_DIM)
    row_t = t0 + lax.broadcasted_iota(i32, (tq, 1), 0)
    row_t4 = t0 + (lax.broadcasted_iota(i32, (R * tq, 1), 0) & (tq - 1))
    per_head = lambda a: jnp.concatenate([a] * R, axis=0)

    ncl = kc_ref.shape[2]
    cl = lax.broadcasted_iota(i32, (1, ncl), 1)
    cmp_end = jnp.where(cl < n_cmp, cl * CMP_STRIDE + (CMP_LEN - 1), jnp.int32(2 ** 30))
    s = lax.dot_general(q, kc_ref[0, 0], NT_DIMS, preferred_element_type=f32) * scale
    s = jnp.where(cmp_end <= row_t4, s, -jnp.inf)
    m = jnp.max(s, axis=-1, keepdims=True)
    m = jnp.where(m == -jnp.inf, 0.0, m)
    p = jnp.exp(s - m)
    d = jnp.sum(p, axis=-1, keepdims=True)
    p = p / jnp.where(d > 0, d, 1.0)
    o_cmp = jnp.dot(p.astype(bf16), vc_ref[0, 0], preferred_element_type=f32)

    psum = p[0:tq]
    for r in range(1, R):
        psum = psum + p[r * tq:(r + 1) * tq]
    p_hi = psum.astype(bf16)
    p_lo = (psum - p_hi.astype(f32)).astype(bf16)
    imp = jnp.dot(p_hi, ov_ref[...], preferred_element_type=f32) + jnp.dot(p_lo, ov_ref[...], preferred_element_type=f32)
    jb = lax.broadcasted_iota(i32, (1, LANES), 1)
    cur = row_t // SLC_LEN
    imp = jnp.where(jb * SLC_LEN <= row_t, imp, -FORCED_SCORE)
    imp = jnp.where(jb == 0, FORCED_SCORE, jnp.where(jb == cur, FORCED_SCORE, jnp.where(jb == cur - 1, FORCED_SCORE, imp)))
    imp = jnp.where(jb < n_slc, imp, -jnp.inf)
    rank = jnp.zeros((tq, LANES), f32)
    for i in range(n_slc):
        col = imp[:, i:i + 1]
        later = jnp.where(jb > i, 1.0, 0.0)
        rank = rank + jnp.where(col > imp, 1.0, jnp.where(col == imp, later, 0.0))
    sel = jnp.where(rank < n_sel, jnp.where(jb < n_slc, 1.0, 0.0), 0.0).astype(bf16)

    def slc_chunk(c, carry):
        m_i, l_i, acc = carry
        k0 = pl.multiple_of(c * tk, tk)
        k = ks_ref[0, 0, pl.ds(k0, tk), :]
        v = vs_ref[0, 0, pl.ds(k0, tk), :]
        sc = lax.dot_general(q, k, NT_DIMS, preferred_element_type=f32) * scale
        kpos = k0 + lax.broadcasted_iota(i32, (1, tk), 1)
        picked = per_head(jnp.dot(sel, ex_ref[c], preferred_element_type=f32))
        sc = jnp.where(kpos <= row_t4, jnp.where(picked > 0.5, sc, NEG_BIG), NEG_BIG)
        m_n = jnp.maximum(m_i, jnp.max(sc, axis=-1, keepdims=True))
        alpha = jnp.exp(m_i - m_n)
        pc = jnp.exp(sc - m_n)
        l_n = alpha * l_i + jnp.sum(pc, axis=-1, keepdims=True)
        acc_n = alpha * acc + jnp.dot(pc.astype(bf16), v, preferred_element_type=f32)
        return m_n, l_n, acc_n

    n_chunks = (t0 + tq + tk - 1) // tk
    init = (jnp.full((R * tq, 1), NEG_BIG, f32), jnp.zeros((R * tq, 1), f32), jnp.zeros((R * tq, HEAD_DIM), f32))
    _, l_s, acc_s = lax.fori_loop(0, n_chunks, slc_chunk, init)
    o_slc = acc_s / l_s

    wk = WINDOW + tq
    w0 = pl.multiple_of(jnp.maximum(t0 - WINDOW, 0), tq)
    kpos = w0 + lax.broadcasted_iota(i32, (1, wk), 1)
    sw = lax.dot_general(q, kw_ref[0, 0, pl.ds(w0, wk), :], NT_DIMS, preferred_element_type=f32) * scale
    sw = jnp.where(kpos <= row_t4, jnp.where(kpos > row_t4 - WINDOW, sw, NEG_BIG), NEG_BIG)
    pw = jnp.exp(sw - jnp.max(sw, axis=-1, keepdims=True))
    o_win = jnp.dot(pw.astype(bf16), vw_ref[0, 0, pl.ds(w0, wk), :], preferred_element_type=f32)
    o_win = o_win / jnp.sum(pw, axis=-1, keepdims=True)

    gates = gate_ref[...]
    for r in range(R):
        rows = slice(r * tq, (r + 1) * tq)
        o_ref[:, r * HEAD_DIM:(r + 1) * HEAD_DIM] = (
            gates[:, 3 * r:3 * r + 1] * o_cmp[rows]
            + gates[:, 3 * r + 1:3 * r + 2] * o_slc[rows]
            + gates[:, 3 * r + 2:3 * r + 3] * o_win[rows])


def _attention(q, kc, vc, ks, vs, kw, vw, gates, *, tq=128, tk=512):
    B, H, S, dh = q.shape
    G = NSA_KV_HEADS
    n_seg = kc.shape[2]
    n_cmp = (S - CMP_LEN) // CMP_STRIDE + 1
    n_slc = S // SLC_LEN
    n_sel = min(N_SELECT, n_slc)
    assert n_seg == LANES and n_slc <= LANES and S % tk == 0 and tk % tq == 0 and S >= WINDOW + tq
    per_b = S // tq
    cs = np.arange(n_seg)[:, None] * CMP_STRIDE
    ss = np.arange(LANES)[None, :] * SLC_LEN
    ov = np.maximum(np.minimum(cs + CMP_LEN, ss + SLC_LEN) - np.maximum(cs, ss), 0).astype(np.float32) / CMP_LEN
    ov[n_cmp:, :] = 0.0
    ov[:, n_slc:] = 0.0
    kpos = np.arange(S).reshape(S // tk, 1, tk)
    ex = (kpos // SLC_LEN == np.arange(LANES)[None, :, None]).astype(np.float32)
    kvspec = pl.BlockSpec((1, 1, S, dh), lambda b, g, i: (b, g, 0, 0))
    cspec = pl.BlockSpec((1, 1, n_seg, dh), lambda b, g, i: (b, g, 0, 0))
    return pl.pallas_call(
        functools.partial(_attn_kernel, tq=tq, tk=tk, n_cmp=n_cmp, n_slc=n_slc, n_sel=n_sel),
        grid=(B, G, per_b),
        in_specs=[
            pl.BlockSpec((1, NSA_GROUP, tq, dh), lambda b, g, i: (b, g, i, 0)),
            cspec, cspec, kvspec, kvspec, kvspec, kvspec,
            pl.BlockSpec((tq, LANES), lambda b, g, i: (b * per_b + i, g)),
            pl.BlockSpec((n_seg, LANES), lambda b, g, i: (0, 0)),
            pl.BlockSpec((S // tk, LANES, tk), lambda b, g, i: (0, 0, 0)),
        ],
        out_specs=pl.BlockSpec((tq, NSA_GROUP * dh), lambda b, g, i: (b * per_b + i, g)),
        out_shape=jax.ShapeDtypeStruct((B * S, NSA_WIDTH), f32),
        compiler_params=_cparams("arbitrary", "arbitrary", "arbitrary"),
        name="nsa_attention",
    )(q, kc, vc, ks, vs, kw, vw, gates, jnp.asarray(ov, bf16), jnp.asarray(ex, bf16))


def _gelu(x):
    return 0.5 * x * (1.0 + lax.erf(x * (2.0 ** -0.5)))


def _rglru_kernel(xr_ref, xg_ref, cw_ref, cb_ref, wa_ref, ba_ref, wi_ref, bi_ref, lam_ref, o_ref,
                  ext_ref, a_ref, b_ref, hs_ref, hc_ref, *, ts):
    halo = 8

    @pl.when(pl.program_id(1) == 0)
    def _():
        ext_ref[0:halo] = jnp.zeros((halo, RNN_WIDTH), f32)
        hc_ref[...] = jnp.zeros_like(hc_ref)

    xr = xr_ref[...]
    ext_ref[halo:halo + ts] = xr
    u = cb_ref[...] + cw_ref[CONV_WIDTH - 1:CONV_WIDTH] * xr
    for k in range(1, CONV_WIDTH):
        u = u + cw_ref[CONV_WIDTH - 1 - k:CONV_WIDTH - k] * ext_ref[halo - k:halo - k + ts]
    ext_ref[0:halo] = xr[ts - halo:ts]

    ra, ri = [], []
    for n in range(RNN_BLOCKS):
        ub = u[:, n * RNN_BLOCK_DIM:(n + 1) * RNN_BLOCK_DIM].astype(bf16)
        ra.append(jnp.dot(ub, wa_ref[n], preferred_element_type=f32))
        ri.append(jnp.dot(ub, wi_ref[n], preferred_element_type=f32))
    r = jax.nn.sigmoid(jnp.concatenate(ra, axis=1) + ba_ref[...])
    ig = jax.nn.sigmoid(jnp.concatenate(ri, axis=1) + bi_ref[...])
    log_a = (-LRU_C) * r * jax.nn.softplus(-lam_ref[...])
    a_ref[...] = jnp.exp(log_a)
    b_ref[...] = jnp.sqrt(1.0 - jnp.exp(2.0 * log_a)) * ig * u

    sub = lax.broadcasted_iota(i32, (8, RNN_WIDTH), 0)

    def blk(j, h):
        r0 = pl.multiple_of(j * 8, 8)
        a8 = a_ref[pl.ds(r0, 8), :]
        b8 = b_ref[pl.ds(r0, 8), :]
        out = jnp.zeros((8, RNN_WIDTH), f32)
        for i in range(8):
            h = a8[i:i + 1] * h + b8[i:i + 1]
            out = jnp.where(sub == i, h, out)
        hs_ref[pl.ds(r0, 8), :] = out
        return h

    hc_ref[...] = lax.fori_loop(0, ts // 8, blk, hc_ref[...])
    o_ref[...] = _gelu(xg_ref[...]) * hs_ref[...]


def _rglru(z, conv_w, conv_b, wa, ba, wi, bi, lam, B, S, *, ts=512):
    W = RNN_WIDTH
    per_b = S // ts
    xr_idx = NSA_WIDTH // W
    vec = lambda a: a.reshape(1, W)
    full = lambda shape: pl.BlockSpec(shape, lambda b, i: (0,) * len(shape))
    return pl.pallas_call(
        functools.partial(_rglru_kernel, ts=ts),
        grid=(B, per_b),
        in_specs=[
            pl.BlockSpec((ts, W), lambda b, i: (b * per_b + i, xr_idx)),
            pl.BlockSpec((ts, W), lambda b, i: (b * per_b + i, xr_idx + 1)),
            full((CONV_WIDTH, W)), full((1, W)),
            full((RNN_BLOCKS, RNN_BLOCK_DIM, RNN_BLOCK_DIM)), full((1, W)),
            full((RNN_BLOCKS, RNN_BLOCK_DIM, RNN_BLOCK_DIM)), full((1, W)), full((1, W)),
        ],
        out_specs=pl.BlockSpec((ts, W), lambda b, i: (b * per_b + i, 0)),
        out_shape=jax.ShapeDtypeStruct((B * S, W), f32),
        scratch_shapes=[pltpu.VMEM((ts + 8, W), f32), pltpu.VMEM((ts, W), f32), pltpu.VMEM((ts, W), f32),
                        pltpu.VMEM((ts, W), f32), pltpu.VMEM((1, W), f32)],
        compiler_params=_cparams("arbitrary", "arbitrary"),
        name="rglru",
    )(z, z, conv_w, vec(conv_b), wa.astype(bf16), vec(ba), wi.astype(bf16), vec(bi), vec(lam))


def _outproj_kernel(oa_ref, or_ref, ga_ref, gr_ref, w_ref, x_ref, gt_ref, o_ref, yb_ref):
    @pl.when(pl.program_id(1) == 0)
    def _():
        wa = oa_ref.shape[1]
        yb_ref[:, 0:wa] = _rms(oa_ref[...], ga_ref[...]).astype(bf16)
        yb_ref[:, wa:] = _rms(or_ref[...], gr_ref[...]).astype(bf16)

    o_ref[...] = x_ref[...] + gt_ref[0] * jnp.dot(yb_ref[...], w_ref[...], preferred_element_type=f32)


def _outproj(o_attn, o_rnn, g_a, g_r, w, x, gt, seq, *, tm=512, tn=1024):
    T, D = x.shape
    B = gt.shape[0]
    wa, wr = o_attn.shape[1], o_rnn.shape[1]
    per_b = seq // tm
    return pl.pallas_call(
        _outproj_kernel,
        grid=(T // tm, D // tn),
        in_specs=[
            pl.BlockSpec((tm, wa), lambda i, j: (i, 0)),
            pl.BlockSpec((tm, wr), lambda i, j: (i, 0)),
            pl.BlockSpec((1, wa), lambda i, j: (0, 0)),
            pl.BlockSpec((1, wr), lambda i, j: (0, 0)),
            pl.BlockSpec((wa + wr, tn), lambda i, j: (0, j)),
            pl.BlockSpec((tm, tn), lambda i, j: (i, j)),
            pl.BlockSpec((1, 1, tn), lambda i, j: (i // per_b, 0, j)),
        ],
        out_specs=pl.BlockSpec((tm, tn), lambda i, j: (i, j)),
        out_shape=jax.ShapeDtypeStruct((T, D), f32),
        scratch_shapes=[pltpu.VMEM((tm, wa + wr), bf16)],
        compiler_params=_cparams("arbitrary", "arbitrary"),
        name="out_proj",
    )(o_attn, o_rnn, g_a.reshape(1, wa), g_r.reshape(1, wr), w, x, gt.reshape(B, 1, D))


def _top_rows(s, k):
    n = s.shape[0]
    io = lax.broadcasted_iota(i32, s.shape, 0)
    vals, idxs = [], []
    for _ in range(k):
        m = jnp.max(s, axis=0, keepdims=True)
        ix = jnp.min(jnp.where(s == m, io, n), axis=0, keepdims=True)
        vals.append(m)
        idxs.append(ix)
        s = jnp.where(io == ix, -jnp.inf, s)
    return vals, idxs


def _peer_topk_kernel(q_ref, keys_ref, e_ref, g_ref):
    K = PEER_TOPK
    tt = q_ref.shape[0]
    pairs = [(a, b) for a in range(K) for b in range(K) if (a + 1) * (b + 1) <= K]
    n_cand = len(pairs) + (-len(pairs) % 8)
    cio = lax.broadcasted_iota(i32, (n_cand, tt), 0)
    oio = lax.broadcasted_iota(i32, (PEER_HEADS * K, tt), 0)
    e_all = jnp.zeros((PEER_HEADS * K, tt), i32)
    g_all = jnp.zeros((PEER_HEADS * K, tt), f32)
    for h in range(PEER_HEADS):
        tops = []
        for p in range(2):
            c0 = (2 * h + p) * PEER_KEYS
            qhp = q_ref[:, c0:c0 + PEER_KEYS].astype(bf16)
            s = lax.dot_general(keys_ref[h, p], qhp, NT_DIMS, preferred_element_type=f32)
            tops.append(_top_rows(s, K))
        (v1, i1), (v2, i2) = tops
        cand = jnp.full((n_cand, tt), -jnp.inf, f32)
        cidx = jnp.zeros((n_cand, tt), i32)
        for r, (a, b) in enumerate(pairs):
            cand = jnp.where(cio == r, v1[a] + v2[b], cand)
            cidx = jnp.where(cio == r, i1[a] * PEER_KEYS + i2[b], cidx)
        vals, pos = _top_rows(cand, K)
        ex = [jnp.exp(v - vals[0]) for v in vals]
        den = functools.reduce(lambda x, y: x + y, ex)
        for r in range(K):
            e_r = jnp.sum(jnp.where(cio == pos[r], cidx, 0), axis=0, keepdims=True)
            e_all = jnp.where(oio == h * K + r, e_r, e_all)
            g_all = jnp.where(oio == h * K + r, ex[r] / den, g_all)
    e_ref[...] = e_all.T
    g_ref[...] = g_all.T


def _peer_topk(qp, keys, *, tt=256):
    T, N = qp.shape
    HK = PEER_HEADS * PEER_TOPK
    return pl.pallas_call(
        _peer_topk_kernel,
        grid=(T // tt,),
        in_specs=[pl.BlockSpec((tt, N), lambda i: (i, 0)),
                  pl.BlockSpec(keys.shape, lambda i: (0, 0, 0, 0))],
        out_specs=[pl.BlockSpec((tt, HK), lambda i: (i, 0))] * 2,
        out_shape=[jax.ShapeDtypeStruct((T, HK), i32), jax.ShapeDtypeStruct((T, HK), f32)],
        compiler_params=_cparams("arbitrary"),
        name="peer_topk",
    )(qp, keys.astype(bf16))


PEER_GROUP = 2
PEER_AHEAD = 3
PEER_SLOTS = PEER_GROUP * (PEER_AHEAD + 1)


def _peer_ffn_kernel(idx_ref, g_ref, h_ref, x_ref, gt_ref, tab_ref, o_ref, *scratch, tt):
    bufs, sem_ref = scratch[:PEER_SLOTS], scratch[PEER_SLOTS]
    NE = idx_ref.shape[1]
    D = h_ref.shape[1]
    n_lt = D // LANES
    per_step = NE // (2 * n_lt)
    ahead_tokens = PEER_GROUP * PEER_AHEAD
    step = pl.program_id(0)

    def issue(t, slot, ks):
        for k in ks:
            pltpu.make_async_copy(tab_ref.at[idx_ref[t, k]], bufs[slot].at[:, k, :], sem_ref.at[slot]).start(priority=k % 2)

    def wait(slot):
        pltpu.make_async_copy(bufs[slot], bufs[slot], sem_ref.at[slot]).wait()

    diag = lax.broadcasted_iota(i32, (NE, LANES), 0) == lax.broadcasted_iota(i32, (NE, LANES), 1)

    def evaluate(t, slot, ahead):
        nslot = (slot + ahead_tokens) % PEER_SLOTS

        def step_done(i):
            if ahead:
                issue(t + ahead_tokens, nslot, range(i * per_step, (i + 1) * per_step))

        xrow = h_ref[pl.ds(t, 1), :]
        part = jnp.zeros((NE, LANES), f32)
        for c in range(n_lt):
            cols = slice(c * LANES, (c + 1) * LANES)
            part = part + pltpu.bitcast(bufs[slot][c] << 16, f32) * xrow[:, cols]
            step_done(c)
        g_col = jnp.sum(jnp.where(diag, g_ref[pl.ds(t, 1), :], 0.0), axis=1, keepdims=True)
        act = jnp.sum(part, axis=1, keepdims=True)
        w_col = jnp.broadcast_to(_gelu(act) * g_col, (NE, LANES))
        outs = []
        for c in range(n_lt):
            up = pltpu.bitcast(bufs[slot][c] & jnp.uint32(0xFFFF0000), f32)
            outs.append(jnp.sum(up * w_col, axis=0, keepdims=True))
            step_done(n_lt + c)
        y = jnp.concatenate(outs, axis=1)
        o_ref[pl.ds(t, 1), :] = x_ref[pl.ds(t, 1), :] + gt_ref[0] * y

    def group(t0, s0, ahead):
        for u in range(PEER_GROUP):
            wait(s0 + u)
        for u in range(PEER_GROUP):
            evaluate(t0 + u, s0 + u, ahead)

    def prime(_, carry):
        for t in range(ahead_tokens):
            issue(t, t, range(NE))
        return carry

    lax.fori_loop(0, jnp.where(step == 0, 1, 0), prime, 0)

    def steady(j, carry):
        for s0 in range(0, PEER_SLOTS, PEER_GROUP):
            group(j * PEER_SLOTS + s0, s0, True)
        return carry

    lax.fori_loop(0, tt // PEER_SLOTS - 1, steady, 0)
    for s0 in range(0, PEER_SLOTS, PEER_GROUP):
        group(tt - PEER_SLOTS + s0, s0, True)

    @pl.when(step == pl.num_programs(0) - 1)
    def _():
        for t in range(ahead_tokens):
            wait(t)


def _peer_ffn(eidx, gsm, h2, x1, gt2, table, seq, *, tt=128):
    T, D = h2.shape
    B = gt2.shape[0]
    NE = eidx.shape[1]
    ahead_tokens = PEER_GROUP * PEER_AHEAD
    assert tt % PEER_SLOTS == 0 and tt >= 2 * PEER_SLOTS and NE % (2 * D // LANES) == 0 and ahead_tokens <= SUBLANES
    per_b = seq // tt
    eidx_ext = jnp.concatenate([eidx, eidx[T - SUBLANES:]], axis=0)
    return pl.pallas_call(
        functools.partial(_peer_ffn_kernel, tt=tt),
        grid=(T // tt,),
        in_specs=[
            pl.BlockSpec((pl.Element(tt + SUBLANES), pl.Element(NE)), lambda i: (i * tt, 0), memory_space=pltpu.SMEM),
            pl.BlockSpec((tt, NE), lambda i: (i, 0)),
            pl.BlockSpec((tt, D), lambda i: (i, 0)),
            pl.BlockSpec((tt, D), lambda i: (i, 0)),
            pl.BlockSpec((1, 1, D), lambda i: (i // per_b, 0, 0)),
            pl.BlockSpec(memory_space=pl.ANY),
        ],
        out_specs=pl.BlockSpec((tt, D), lambda i: (i, 0)),
        out_shape=jax.ShapeDtypeStruct((T, D), f32),
        scratch_shapes=([pltpu.VMEM((D // LANES, NE, LANES), jnp.uint32)] * PEER_SLOTS
                        + [pltpu.SemaphoreType.DMA((PEER_SLOTS,))]),
        compiler_params=_cparams("arbitrary"),
        name="peer_ffn",
    )(eidx_ext, gsm, h2, x1, gt2.reshape(B, 1, D), table)


def _pack_kernel(d_ref, u_ref, o_ref):
    lo = pltpu.bitcast(d_ref[...].astype(bf16).astype(f32), jnp.uint32) >> 16
    hi = pltpu.bitcast(u_ref[...].astype(bf16).astype(f32), jnp.uint32)
    packed = lo | hi
    for c in range(o_ref.shape[1]):
        o_ref[:, c, :] = packed[:, c * LANES:(c + 1) * LANES]


def _pack_experts(down, up, *, tb=256):
    E, D = down.shape
    spec = pl.BlockSpec((tb, D), lambda i: (i, 0))
    return pl.pallas_call(
        _pack_kernel,
        grid=(E // tb,),
        in_specs=[spec, spec],
        out_specs=pl.BlockSpec((tb, D // LANES, LANES), lambda i: (i, 0, 0)),
        out_shape=jax.ShapeDtypeStruct((E, D // LANES, LANES), jnp.uint32),
        compiler_params=_cparams("arbitrary"),
        name="pack_experts",
    )(down, up)


def _rope_tables(S):
    half = HEAD_DIM // 2
    freqs = ROPE_THETA ** (-jnp.arange(half, dtype=f32) / half)
    ang = jnp.arange(S, dtype=f32)[:, None] * freqs[None, :]
    cos, sin = jnp.cos(ang), jnp.sin(ang)
    return jnp.concatenate([cos, cos], axis=1), jnp.concatenate([-sin, sin], axis=1)


def _layer(x, c, ada_w, ada_b, norm_mix_g, norm_ffn_g, w_in, w_out, q_norm_g, k_norm_g, cmp_pe_k, cmp_pe_v,
           cmp_w_k, cmp_w_v, gate_b, conv_w, conv_b, lru_wa, lru_ba, lru_wi, lru_bi, lru_lam, out_g_attn,
           out_g_rnn, peer_wq, peer_keys, peer_down, peer_up):
    B, S, D = x.shape
    T = B * S
    xf = x.reshape(T, D)
    mod = _mod(c, ada_w, ada_b)
    sh1, sc1, gt1, sh2, sc2, gt2 = [mod[:, k * D:(k + 1) * D] for k in range(6)]

    q0, g0 = NSA_WIDTH, NSA_WIDTH + 6 * KV_COLS
    r0 = g0 + GATE_COLS
    kvs = [w_in[:, q0 + k * KV_COLS:q0 + (k + 1) * KV_COLS] for k in range(6)]
    w_main = jnp.concatenate([w_in[:, :q0], w_in[:, r0:r0 + 2 * RNN_WIDTH]] + kvs, axis=1).astype(bf16)
    per_g = GATE_COLS // NSA_KV_HEADS
    wg = jnp.zeros((D, NSA_KV_HEADS * LANES), f32)
    gb = jnp.zeros((1, NSA_KV_HEADS * LANES), f32)
    for g in range(NSA_KV_HEADS):
        wg = wg.at[:, g * LANES:g * LANES + per_g].set(w_in[:, g0 + g * per_g:g0 + (g + 1) * per_g])
        gb = gb.at[0, g * LANES:g * LANES + per_g].set(gate_b[g * per_g:(g + 1) * per_g])
    z, gates = _norm_mm(xf, sc1, sh1, norm_mix_g, w_main, S, tm=512, tn=768, wg=wg.astype(bf16), gb=gb)

    cos, sin = _rope_tables(S)
    q, kc_r, vc_r, ks, vs, kw, vw = _prep(z, cos, sin, q_norm_g, k_norm_g, B, S)
    kc, vc = _compress(kc_r, vc_r, cmp_pe_k, cmp_pe_v, cmp_w_k, cmp_w_v, k_norm_g[0])
    o_attn = _attention(q, kc, vc, ks, vs, kw, vw, gates)
    o_rnn = _rglru(z, conv_w, conv_b, lru_wa, lru_ba, lru_wi, lru_bi, lru_lam, B, S)
    x1 = _outproj(o_attn, o_rnn, out_g_attn, out_g_rnn, w_out.astype(bf16), xf, gt1, S)

    qp, h2 = _norm_mm(x1, sc2, sh2, norm_ffn_g, peer_wq.astype(bf16), S, tm=512, tn=1024, emit_h=True)
    eidx, gsm = _peer_topk(qp, peer_keys)
    out = _peer_ffn(eidx, gsm, h2, x1, gt2, _pack_experts(peer_down, peer_up), S)
    return out.reshape(B, S, D)


def kernel(x, c, ada_w, ada_b, norm_mix_g, norm_ffn_g, w_in, w_out, q_norm_g, k_norm_g, cmp_pe_k, cmp_pe_v, cmp_w_k, cmp_w_v, gate_b, conv_w, conv_b, lru_wa, lru_ba, lru_wi, lru_bi, lru_lam, out_g_attn, out_g_rnn, peer_wq, peer_keys, peer_down, peer_up):
    params = (ada_w, ada_b, norm_mix_g, norm_ffn_g, w_in, w_out, q_norm_g, k_norm_g, cmp_pe_k, cmp_pe_v, cmp_w_k,
              cmp_w_v, gate_b, conv_w, conv_b, lru_wa, lru_ba, lru_wi, lru_bi, lru_lam, out_g_attn, out_g_rnn,
              peer_wq, peer_keys, peer_down, peer_up)
    for l in range(ada_w.shape[0]):
        x = _layer(x, c, *[p[l] for p in params])
    return x
```

```python
import functools

import numpy as np
import jax
import jax.numpy as jnp
from jax import lax
from jax.experimental import pallas as pl
from jax.experimental.pallas import tpu as pltpu

f32 = jnp.float32
bf16 = jnp.bfloat16
i32 = jnp.int32

HEAD_DIM = 128
NSA_HEADS = 8
NSA_KV_HEADS = 2
NSA_GROUP = NSA_HEADS // NSA_KV_HEADS
NSA_WIDTH = NSA_HEADS * HEAD_DIM
KV_COLS = NSA_KV_HEADS * HEAD_DIM
GATE_COLS = 3 * NSA_HEADS
CMP_LEN = 32
CMP_STRIDE = 16
SLC_LEN = 64
N_SELECT = 16
FORCED_SCORE = 1e4
WINDOW = 512
RNN_WIDTH = 1024
RNN_BLOCKS = 8
RNN_BLOCK_DIM = RNN_WIDTH // RNN_BLOCKS
CONV_WIDTH = 4
LRU_C = 8.0
PEER_HEADS = 8
PEER_KEYS = 128
PEER_TOPK = 16
ROPE_THETA = 10000.0
EPS = 1e-6
NEG_BIG = -1e30

LANES = 128
SUBLANES = 8
VMEM_LIMIT = 56 * 1024 * 1024

NT_DIMS = (((1,), (1,)), ((), ()))


def _cparams(*sem):
    return pltpu.CompilerParams(dimension_semantics=sem, vmem_limit_bytes=VMEM_LIMIT)


def _rms(x, g):
    return x * lax.rsqrt(jnp.mean(x * x, axis=-1, keepdims=True) + EPS) * g


def _mod_kernel(c_ref, w_ref, b_ref, o_ref):
    c = c_ref[...]
    a = c * jax.nn.sigmoid(c)
    o_ref[...] = jnp.dot(a, w_ref[...], preferred_element_type=f32) + b_ref[...]


def _mod(c, w, b, tn=1024):
    B, D = c.shape
    N = w.shape[1]
    return pl.pallas_call(
        _mod_kernel,
        grid=(N // tn,),
        in_specs=[
            pl.BlockSpec((B, D), lambda j: (0, 0)),
            pl.BlockSpec((D, tn), lambda j: (0, j)),
            pl.BlockSpec((1, tn), lambda j: (0, j)),
        ],
        out_specs=pl.BlockSpec((B, tn), lambda j: (0, j)),
        out_shape=jax.ShapeDtypeStruct((B, N), f32),
        compiler_params=_cparams("arbitrary"),
        name="mod",
    )(c, w, b.reshape(1, N))


def _norm_mm_kernel(x_ref, sc_ref, sh_ref, g_ref, w_ref, *rest, n_extra, emit_h):
    rest = list(rest)
    if n_extra:
        wg_ref, gb_ref = rest.pop(0), rest.pop(0)
    z_ref = rest.pop(0)
    if n_extra:
        gate_ref = rest.pop(0)
    if emit_h:
        h_ref = rest.pop(0)
    hb_ref = rest.pop(0)

    @pl.when(pl.program_id(1) == 0)
    def _():
        h = _rms(x_ref[...], g_ref[...]) * (1.0 + sc_ref[0]) + sh_ref[0]
        hb = h.astype(bf16)
        hb_ref[...] = hb
        if emit_h:
            h_ref[...] = h
        if n_extra:
            gl = jnp.dot(hb, wg_ref[...], preferred_element_type=f32)
            gate_ref[...] = jax.nn.sigmoid(gl + gb_ref[...])

    z_ref[...] = jnp.dot(hb_ref[...], w_ref[...], preferred_element_type=f32)


def _norm_mm(x, sc, sh, g, w, seq, *, tm, tn, wg=None, gb=None, emit_h=False):
    T, D = x.shape
    N = w.shape[1]
    B = sc.shape[0]
    assert seq % tm == 0 and N % tn == 0
    per_b = seq // tm
    n_extra = 0 if wg is None else wg.shape[1]
    in_specs = [
        pl.BlockSpec((tm, D), lambda i, j: (i, 0)),
        pl.BlockSpec((1, 1, D), lambda i, j: (i // per_b, 0, 0)),
        pl.BlockSpec((1, 1, D), lambda i, j: (i // per_b, 0, 0)),
        pl.BlockSpec((1, D), lambda i, j: (0, 0)),
        pl.BlockSpec((D, tn), lambda i, j: (0, j)),
    ]
    args = [x, sc.reshape(B, 1, D), sh.reshape(B, 1, D), g.reshape(1, D), w]
    out_specs = [pl.BlockSpec((tm, tn), lambda i, j: (i, j))]
    out_shape = [jax.ShapeDtypeStruct((T, N), f32)]
    if n_extra:
        in_specs += [pl.BlockSpec((D, n_extra), lambda i, j: (0, 0)), pl.BlockSpec((1, n_extra), lambda i, j: (0, 0))]
        args += [wg, gb]
        out_specs.append(pl.BlockSpec((tm, n_extra), lambda i, j: (i, 0)))
        out_shape.append(jax.ShapeDtypeStruct((T, n_extra), f32))
    if emit_h:
        out_specs.append(pl.BlockSpec((tm, D), lambda i, j: (i, 0)))
        out_shape.append(jax.ShapeDtypeStruct((T, D), f32))
    return pl.pallas_call(
        functools.partial(_norm_mm_kernel, n_extra=n_extra, emit_h=emit_h),
        grid=(T // tm, N // tn),
        in_specs=in_specs,
        out_specs=out_specs,
        out_shape=out_shape,
        scratch_shapes=[pltpu.VMEM((tm, D), bf16)],
        compiler_params=_cparams("arbitrary", "arbitrary"),
        name="norm_mm",
    )(*args)


def _prep_kernel(zq_ref, zc_ref, zs_ref, zw_ref, cos_ref, sin_ref, qg_ref, kg_ref,
                 q_ref, kc_ref, vc_ref, ks_ref, vs_ref, kw_ref, vw_ref):
    cos = cos_ref[...]
    sin = sin_ref[...]

    def rope(y):
        return y * cos + pltpu.roll(y, HEAD_DIM // 2, 1) * sin

    def head(ref, h):
        return ref[:, h * HEAD_DIM:(h + 1) * HEAD_DIM]

    qg = qg_ref[...]
    for h in range(NSA_HEADS):
        q_ref[0, h] = rope(_rms(head(zq_ref, h), qg)).astype(bf16)
    for g in range(NSA_KV_HEADS):
        kc_ref[0, g] = rope(head(zc_ref, g))
        vc_ref[0, g] = head(zc_ref, NSA_KV_HEADS + g)
        ks_ref[0, g] = rope(_rms(head(zs_ref, g), kg_ref[1:2])).astype(bf16)
        vs_ref[0, g] = head(zs_ref, NSA_KV_HEADS + g).astype(bf16)
        kw_ref[0, g] = rope(_rms(head(zw_ref, g), kg_ref[2:3])).astype(bf16)
        vw_ref[0, g] = head(zw_ref, NSA_KV_HEADS + g).astype(bf16)


def _prep(z, cos, sin, q_g, k_g, B, S, *, ts=256):
    per_b = S // ts
    G = NSA_KV_HEADS
    kv = 2 * KV_COLS
    base = (NSA_WIDTH + 2 * RNN_WIDTH) // kv

    def zspec(width, idx):
        return pl.BlockSpec((ts, width), lambda b, i: (b * per_b + i, idx))

    def hm(nh, dt):
        return pl.BlockSpec((1, nh, ts, HEAD_DIM), lambda b, i: (b, 0, i, 0)), jax.ShapeDtypeStruct((B, nh, S, HEAD_DIM), dt)

    outs = [hm(NSA_HEADS, bf16), hm(G, f32), hm(G, f32), hm(G, bf16), hm(G, bf16), hm(G, bf16), hm(G, bf16)]
    return pl.pallas_call(
        _prep_kernel,
        grid=(B, per_b),
        in_specs=[
            zspec(NSA_WIDTH, 0), zspec(kv, base), zspec(kv, base + 1), zspec(kv, base + 2),
            pl.BlockSpec((ts, HEAD_DIM), lambda b, i: (i, 0)),
            pl.BlockSpec((ts, HEAD_DIM), lambda b, i: (i, 0)),
            pl.BlockSpec((1, HEAD_DIM), lambda b, i: (0, 0)),
            pl.BlockSpec((3, HEAD_DIM), lambda b, i: (0, 0)),
        ],
        out_specs=[o[0] for o in outs],
        out_shape=[o[1] for o in outs],
        compiler_params=_cparams("arbitrary", "arbitrary"),
        name="prep",
    )(z, z, z, z, cos, sin, q_g.reshape(1, HEAD_DIM), k_g)


def _compress_kernel(kseg_ref, vseg_ref, pek_ref, pev_ref, wk_ref, wv_ref, kg_ref, kc_ref, vc_ref, *, n_cmp):
    n_seg = kseg_ref.shape[2]
    half = CMP_STRIDE * HEAD_DIM
    row = lax.broadcasted_iota(i32, (n_seg, HEAD_DIM), 0)

    def comp(seg_ref, pe_ref, w_ref):
        seg = seg_ref[0, 0]
        lo = jnp.dot((seg + pe_ref[0:1]).astype(bf16), w_ref[0:half], preferred_element_type=f32)
        hi = jnp.dot((seg + pe_ref[1:2]).astype(bf16), w_ref[half:2 * half], preferred_element_type=f32)
        return lo + pltpu.roll(hi, n_seg - 1, 0)

    kc = _rms(comp(kseg_ref, pek_ref, wk_ref), kg_ref[...])
    vc = comp(vseg_ref, pev_ref, wv_ref)
    kc_ref[0, 0] = jnp.where(row < n_cmp, kc, 0.0).astype(bf16)
    vc_ref[0, 0] = jnp.where(row < n_cmp, vc, 0.0).astype(bf16)


def _compress(kc_r, vc_r, pe_k, pe_v, w_k, w_v, k_g0):
    B, G, S, dh = kc_r.shape
    n_seg = S // CMP_STRIDE
    n_cmp = (S - CMP_LEN) // CMP_STRIDE + 1
    half = CMP_STRIDE * dh
    seg = lambda a: a.reshape(B, G, n_seg, half)
    segspec = pl.BlockSpec((1, 1, n_seg, half), lambda b, g: (b, g, 0, 0))
    full = lambda shape: pl.BlockSpec(shape, lambda b, g: (0,) * len(shape))
    ospec = pl.BlockSpec((1, 1, n_seg, dh), lambda b, g: (b, g, 0, 0))
    return pl.pallas_call(
        functools.partial(_compress_kernel, n_cmp=n_cmp),
        grid=(B, G),
        in_specs=[segspec, segspec, full((2, half)), full((2, half)), full((2 * half, dh)), full((2 * half, dh)),
                  full((1, dh))],
        out_specs=[ospec, ospec],
        out_shape=[jax.ShapeDtypeStruct((B, G, n_seg, dh), bf16)] * 2,
        compiler_params=_cparams("arbitrary", "arbitrary"),
        name="compress",
    )(seg(kc_r), seg(vc_r), pe_k.reshape(2, half), pe_v.reshape(2, half), w_k.astype(bf16), w_v.astype(bf16),
      k_g0.reshape(1, dh))


def _attn_kernel(q_ref, kc_ref, vc_ref, ks_ref, vs_ref, kw_ref, vw_ref, gate_ref, ovt_ref, ex_ref, cb_ref, wb_ref,
                 o_ref, *, tq, tk, n_cmp, n_slc, n_sel):
    R = NSA_GROUP
    scale = HEAD_DIM ** -0.5
    exp2_scale = scale * 1.4426950408889634
    step = pl.program_id(2)
    t0 = step * tq
    q = q_ref[0].reshape(R * tq, HEAD_DIM)
    row_t4 = t0 + (lax.broadcasted_iota(i32, (R * tq, 1), 0) & (tq - 1))
    per_head = lambda a: jnp.concatenate([a] * R, axis=0)

    ncl = kc_ref.shape[2]
    cl = lax.broadcasted_iota(i32, (1, ncl), 1)
    cmp_end = jnp.where(cl < n_cmp, cl * CMP_STRIDE + (CMP_LEN - 1), jnp.int32(2 ** 30))
    s = lax.dot_general(q, kc_ref[0, 0], NT_DIMS, preferred_element_type=f32) * scale
    s = jnp.where(cmp_end <= row_t4, s, -jnp.inf)
    m = jnp.max(s, axis=-1, keepdims=True)
    m = jnp.where(m == -jnp.inf, 0.0, m)
    p = jnp.exp(s - m)
    d = jnp.sum(p, axis=-1, keepdims=True)
    p = p / jnp.where(d > 0, d, 1.0)
    o_cmp = jnp.dot(p.astype(bf16), vc_ref[0, 0], preferred_element_type=f32)

    psum = p[0:tq]
    for r in range(1, R):
        psum = psum + p[r * tq:(r + 1) * tq]
    p_hi = psum.astype(bf16)
    p_lo = (psum - p_hi.astype(f32)).astype(bf16)
    ovt = ovt_ref[...]
    imp = (lax.dot_general(ovt, p_hi, NT_DIMS, preferred_element_type=f32)
           + lax.dot_general(ovt, p_lo, NT_DIMS, preferred_element_type=f32))
    nb = ovt.shape[0]
    jb = lax.broadcasted_iota(i32, (nb, tq), 0)
    tcol = t0 + lax.broadcasted_iota(i32, (nb, tq), 1)
    cur = tcol // SLC_LEN
    imp = jnp.where(jb * SLC_LEN <= tcol, imp, -FORCED_SCORE)
    imp = jnp.where(jb == 0, FORCED_SCORE, jnp.where(jb == cur, FORCED_SCORE, jnp.where(jb == cur - 1, FORCED_SCORE, imp)))
    imp = jnp.where(jb < n_slc, imp, -jnp.inf)
    rank = jnp.zeros((nb, tq), f32)
    for i in range(n_slc):
        row = imp[i:i + 1, :]
        later = jnp.where(jb > i, 1.0, 0.0)
        rank = rank + jnp.where(row > imp, 1.0, jnp.where(row == imp, later, 0.0))
    sel_bias = jnp.where(rank < n_sel, jnp.where(jb < n_slc, 0.0, NEG_BIG), NEG_BIG)
    sel_bias = jnp.concatenate([sel_bias, jnp.full((LANES - nb, tq), NEG_BIG, f32)], axis=0).T.astype(bf16)

    def slc_chunk(c, carry, causal_bias):
        m_i, l_i, acc = carry
        k0 = pl.multiple_of(c * tk, tk)
        k = ks_ref[0, 0, pl.ds(k0, tk), :]
        v = vs_ref[0, 0, pl.ds(k0, tk), :]
        bias = jnp.dot(sel_bias, ex_ref[c], preferred_element_type=f32)
        if causal_bias is not None:
            bias = bias + causal_bias
        sc = lax.dot_general(q, k, NT_DIMS, preferred_element_type=f32) + per_head(bias)
        m_n = jnp.maximum(m_i, jnp.max(sc, axis=-1, keepdims=True))
        alpha = jnp.exp2((m_i - m_n) * exp2_scale)
        pc = jnp.exp2((sc - m_n) * exp2_scale)
        l_n = alpha * l_i + jnp.sum(pc, axis=-1, keepdims=True)
        acc_n = alpha * acc + jnp.dot(pc.astype(bf16), v, preferred_element_type=f32)
        return m_n, l_n, acc_n

    last = (t0 + tq - 1) // tk
    init = (jnp.full((R * tq, 1), NEG_BIG, f32), jnp.zeros((R * tq, 1), f32), jnp.zeros((R * tq, HEAD_DIM), f32))
    carry = lax.fori_loop(0, last, lambda c, cr: slc_chunk(c, cr, None), init)
    _, l_s, acc_s = slc_chunk(last, carry, cb_ref[(t0 - last * tk) // tq])
    o_slc = acc_s / l_s

    wk = WINDOW + tq
    w0 = pl.multiple_of(jnp.maximum(t0 - WINDOW, 0), tq)
    sw = lax.dot_general(q, kw_ref[0, 0, pl.ds(w0, wk), :], NT_DIMS, preferred_element_type=f32)
    sw = sw + per_head(wb_ref[jnp.minimum(step, WINDOW // tq)])
    pw = jnp.exp2((sw - jnp.max(sw, axis=-1, keepdims=True)) * exp2_scale)
    o_win = jnp.dot(pw.astype(bf16), vw_ref[0, 0, pl.ds(w0, wk), :], preferred_element_type=f32)
    o_win = o_win / jnp.sum(pw, axis=-1, keepdims=True)

    gates = gate_ref[...]
    for r in range(R):
        rows = slice(r * tq, (r + 1) * tq)
        o_ref[:, r * HEAD_DIM:(r + 1) * HEAD_DIM] = (
            gates[:, 3 * r:3 * r + 1] * o_cmp[rows]
            + gates[:, 3 * r + 1:3 * r + 2] * o_slc[rows]
            + gates[:, 3 * r + 2:3 * r + 3] * o_win[rows])


def _attention(q, kc, vc, ks, vs, kw, vw, gates, *, tq=128, tk=512):
    B, H, S, dh = q.shape
    G = NSA_KV_HEADS
    n_seg = kc.shape[2]
    n_cmp = (S - CMP_LEN) // CMP_STRIDE + 1
    n_slc = S // SLC_LEN
    n_sel = min(N_SELECT, n_slc)
    assert n_seg == LANES and n_slc <= LANES and S % tk == 0 and tk % tq == 0 and S >= WINDOW + tq
    per_b = S // tq
    cs = np.arange(n_seg)[:, None] * CMP_STRIDE
    ss = np.arange(LANES)[None, :] * SLC_LEN
    ov = np.maximum(np.minimum(cs + CMP_LEN, ss + SLC_LEN) - np.maximum(cs, ss), 0).astype(np.float32) / CMP_LEN
    ov[n_cmp:, :] = 0.0
    ov[:, n_slc:] = 0.0
    nb = -(-n_slc // SUBLANES) * SUBLANES
    ovt = np.ascontiguousarray(ov.T[:nb])
    kpos = np.arange(S).reshape(S // tk, 1, tk)
    ex = (kpos // SLC_LEN == np.arange(LANES)[None, :, None]).astype(np.float32)
    r_ = np.arange(tq)[None, :, None]
    cb = np.where(np.arange(tk)[None, None, :] <= np.arange(tk // tq)[:, None, None] * tq + r_, 0.0, NEG_BIG)
    wk = WINDOW + tq
    col = np.arange(wk)[None, None, :]
    early = col <= np.arange(WINDOW // tq)[:, None, None] * tq + r_
    full = (col <= r_ + WINDOW) & (col > r_)
    wb = np.where(np.concatenate([early, full], axis=0), 0.0, NEG_BIG)
    const = lambda a: pl.BlockSpec(a.shape, lambda b, g, i: (0,) * a.ndim)
    kvspec = pl.BlockSpec((1, 1, S, dh), lambda b, g, i: (b, g, 0, 0))
    cspec = pl.BlockSpec((1, 1, n_seg, dh), lambda b, g, i: (b, g, 0, 0))
    return pl.pallas_call(
        functools.partial(_attn_kernel, tq=tq, tk=tk, n_cmp=n_cmp, n_slc=n_slc, n_sel=n_sel),
        grid=(B, G, per_b),
        in_specs=[
            pl.BlockSpec((1, NSA_GROUP, tq, dh), lambda b, g, i: (b, g, i, 0)),
            cspec, cspec, kvspec, kvspec, kvspec, kvspec,
            pl.BlockSpec((tq, LANES), lambda b, g, i: (b * per_b + i, g)),
            const(ovt), const(ex), const(cb), const(wb),
        ],
        out_specs=pl.BlockSpec((tq, NSA_GROUP * dh), lambda b, g, i: (b * per_b + i, g)),
        out_shape=jax.ShapeDtypeStruct((B * S, NSA_WIDTH), f32),
        compiler_params=_cparams("arbitrary", "arbitrary", "arbitrary"),
        name="nsa_attention",
    )(q, kc, vc, ks, vs, kw, vw, gates, jnp.asarray(ovt, bf16), jnp.asarray(ex, bf16), jnp.asarray(cb, f32),
      jnp.asarray(wb, f32))


def _gelu(x):
    return 0.5 * x * (1.0 + lax.erf(x * (2.0 ** -0.5)))


def _rglru_kernel(xr_ref, xg_ref, cw_ref, cb_ref, wa_ref, ba_ref, wi_ref, bi_ref, lam_ref, o_ref,
                  ext_ref, a_ref, b_ref, hs_ref, hc_ref, *, ts):
    halo = 8

    @pl.when(pl.program_id(1) == 0)
    def _():
        ext_ref[0:halo] = jnp.zeros((halo, RNN_WIDTH), f32)
        hc_ref[...] = jnp.zeros_like(hc_ref)

    xr = xr_ref[...]
    ext_ref[halo:halo + ts] = xr
    u = cb_ref[...] + cw_ref[CONV_WIDTH - 1:CONV_WIDTH] * xr
    for k in range(1, CONV_WIDTH):
        u = u + cw_ref[CONV_WIDTH - 1 - k:CONV_WIDTH - k] * ext_ref[halo - k:halo - k + ts]
    ext_ref[0:halo] = xr[ts - halo:ts]

    ra, ri = [], []
    for n in range(RNN_BLOCKS):
        ub = u[:, n * RNN_BLOCK_DIM:(n + 1) * RNN_BLOCK_DIM].astype(bf16)
        ra.append(jnp.dot(ub, wa_ref[n], preferred_element_type=f32))
        ri.append(jnp.dot(ub, wi_ref[n], preferred_element_type=f32))
    r = jax.nn.sigmoid(jnp.concatenate(ra, axis=1) + ba_ref[...])
    ig = jax.nn.sigmoid(jnp.concatenate(ri, axis=1) + bi_ref[...])
    log_a = (-LRU_C) * r * jax.nn.softplus(-lam_ref[...])
    a_ref[...] = jnp.exp(log_a)
    b_ref[...] = jnp.sqrt(1.0 - jnp.exp(2.0 * log_a)) * ig * u

    sub = lax.broadcasted_iota(i32, (8, RNN_WIDTH), 0)

    def blk(j, h):
        r0 = pl.multiple_of(j * 8, 8)
        a8 = a_ref[pl.ds(r0, 8), :]
        b8 = b_ref[pl.ds(r0, 8), :]
        out = jnp.zeros((8, RNN_WIDTH), f32)
        for i in range(8):
            h = a8[i:i + 1] * h + b8[i:i + 1]
            out = jnp.where(sub == i, h, out)
        hs_ref[pl.ds(r0, 8), :] = out
        return h

    hc_ref[...] = lax.fori_loop(0, ts // 8, blk, hc_ref[...])
    o_ref[...] = _gelu(xg_ref[...]) * hs_ref[...]


def _rglru(z, conv_w, conv_b, wa, ba, wi, bi, lam, B, S, *, ts=512):
    W = RNN_WIDTH
    per_b = S // ts
    xr_idx = NSA_WIDTH // W
    vec = lambda a: a.reshape(1, W)
    full = lambda shape: pl.BlockSpec(shape, lambda b, i: (0,) * len(shape))
    return pl.pallas_call(
        functools.partial(_rglru_kernel, ts=ts),
        grid=(B, per_b),
        in_specs=[
            pl.BlockSpec((ts, W), lambda b, i: (b * per_b + i, xr_idx)),
            pl.BlockSpec((ts, W), lambda b, i: (b * per_b + i, xr_idx + 1)),
            full((CONV_WIDTH, W)), full((1, W)),
            full((RNN_BLOCKS, RNN_BLOCK_DIM, RNN_BLOCK_DIM)), full((1, W)),
            full((RNN_BLOCKS, RNN_BLOCK_DIM, RNN_BLOCK_DIM)), full((1, W)), full((1, W)),
        ],
        out_specs=pl.BlockSpec((ts, W), lambda b, i: (b * per_b + i, 0)),
        out_shape=jax.ShapeDtypeStruct((B * S, W), f32),
        scratch_shapes=[pltpu.VMEM((ts + 8, W), f32), pltpu.VMEM((ts, W), f32), pltpu.VMEM((ts, W), f32),
                        pltpu.VMEM((ts, W), f32), pltpu.VMEM((1, W), f32)],
        compiler_params=_cparams("arbitrary", "arbitrary"),
        name="rglru",
    )(z, z, conv_w, vec(conv_b), wa.astype(bf16), vec(ba), wi.astype(bf16), vec(bi), vec(lam))


def _outproj_kernel(oa_ref, or_ref, ga_ref, gr_ref, w_ref, x_ref, gt_ref, o_ref, yb_ref):
    @pl.when(pl.program_id(1) == 0)
    def _():
        wa = oa_ref.shape[1]
        yb_ref[:, 0:wa] = _rms(oa_ref[...], ga_ref[...]).astype(bf16)
        yb_ref[:, wa:] = _rms(or_ref[...], gr_ref[...]).astype(bf16)

    o_ref[...] = x_ref[...] + gt_ref[0] * jnp.dot(yb_ref[...], w_ref[...], preferred_element_type=f32)


def _outproj(o_attn, o_rnn, g_a, g_r, w, x, gt, seq, *, tm=512, tn=2048):
    T, D = x.shape
    B = gt.shape[0]
    wa, wr = o_attn.shape[1], o_rnn.shape[1]
    per_b = seq // tm
    return pl.pallas_call(
        _outproj_kernel,
        grid=(T // tm, D // tn),
        in_specs=[
            pl.BlockSpec((tm, wa), lambda i, j: (i, 0)),
            pl.BlockSpec((tm, wr), lambda i, j: (i, 0)),
            pl.BlockSpec((1, wa), lambda i, j: (0, 0)),
            pl.BlockSpec((1, wr), lambda i, j: (0, 0)),
            pl.BlockSpec((wa + wr, tn), lambda i, j: (0, j)),
            pl.BlockSpec((tm, tn), lambda i, j: (i, j)),
            pl.BlockSpec((1, 1, tn), lambda i, j: (i // per_b, 0, j)),
        ],
        out_specs=pl.BlockSpec((tm, tn), lambda i, j: (i, j)),
        out_shape=jax.ShapeDtypeStruct((T, D), f32),
        scratch_shapes=[pltpu.VMEM((tm, wa + wr), bf16)],
        compiler_params=_cparams("arbitrary", "arbitrary"),
        name="out_proj",
    )(o_attn, o_rnn, g_a.reshape(1, wa), g_r.reshape(1, wr), w, x, gt.reshape(B, 1, D))


def _top_rows(s, k):
    n = s.shape[0]
    io = lax.broadcasted_iota(i32, s.shape, 0)
    vals, idxs = [], []
    for _ in range(k):
        m = jnp.max(s, axis=0, keepdims=True)
        ix = jnp.min(jnp.where(s == m, io, n), axis=0, keepdims=True)
        vals.append(m)
        idxs.append(ix)
        s = jnp.where(io == ix, -jnp.inf, s)
    return vals, idxs


def _peer_topk_kernel(q_ref, keys_ref, e_ref, g_ref):
    K = PEER_TOPK
    tt = q_ref.shape[0]
    pairs = [(a, b) for a in range(K) for b in range(K) if (a + 1) * (b + 1) <= K]
    n_cand = len(pairs) + (-len(pairs) % 8)
    cio = lax.broadcasted_iota(i32, (n_cand, tt), 0)
    oio = lax.broadcasted_iota(i32, (PEER_HEADS * K, tt), 0)
    e_all = jnp.zeros((PEER_HEADS * K, tt), i32)
    g_all = jnp.zeros((PEER_HEADS * K, tt), f32)
    for h in range(PEER_HEADS):
        tops = []
        for p in range(2):
            c0 = (2 * h + p) * PEER_KEYS
            qhp = q_ref[:, c0:c0 + PEER_KEYS].astype(bf16)
            s = lax.dot_general(keys_ref[h, p], qhp, NT_DIMS, preferred_element_type=f32)
            tops.append(_top_rows(s, K))
        (v1, i1), (v2, i2) = tops
        cand = jnp.full((n_cand, tt), -jnp.inf, f32)
        cidx = jnp.zeros((n_cand, tt), i32)
        for r, (a, b) in enumerate(pairs):
            cand = jnp.where(cio == r, v1[a] + v2[b], cand)
            cidx = jnp.where(cio == r, i1[a] * PEER_KEYS + i2[b], cidx)
        vals, pos = _top_rows(cand, K)
        ex = [jnp.exp(v - vals[0]) for v in vals]
        den = functools.reduce(lambda x, y: x + y, ex)
        for r in range(K):
            e_r = jnp.sum(jnp.where(cio == pos[r], cidx, 0), axis=0, keepdims=True)
            e_all = jnp.where(oio == h * K + r, e_r, e_all)
            g_all = jnp.where(oio == h * K + r, ex[r] / den, g_all)
    e_ref[...] = e_all.T
    g_ref[...] = g_all.T


def _peer_topk(qp, keys, *, tt=256):
    T, N = qp.shape
    HK = PEER_HEADS * PEER_TOPK
    return pl.pallas_call(
        _peer_topk_kernel,
        grid=(T // tt,),
        in_specs=[pl.BlockSpec((tt, N), lambda i: (i, 0)),
                  pl.BlockSpec(keys.shape, lambda i: (0, 0, 0, 0))],
        out_specs=[pl.BlockSpec((tt, HK), lambda i: (i, 0))] * 2,
        out_shape=[jax.ShapeDtypeStruct((T, HK), i32), jax.ShapeDtypeStruct((T, HK), f32)],
        compiler_params=_cparams("arbitrary"),
        name="peer_topk",
    )(qp, keys.astype(bf16))


PEER_GROUP = 2
PEER_AHEAD = 3
PEER_SLOTS = PEER_GROUP * (PEER_AHEAD + 1)


def _peer_ffn_kernel(idx_ref, g_ref, h_ref, x_ref, gt_ref, tab_ref, o_ref, *scratch, tt):
    bufs, sem_ref = scratch[:PEER_SLOTS], scratch[PEER_SLOTS]
    NE = idx_ref.shape[1]
    D = h_ref.shape[1]
    n_lt = D // LANES
    per_step = NE // (2 * n_lt)
    ahead_tokens = PEER_GROUP * PEER_AHEAD
    step = pl.program_id(0)

    def issue(t, slot, ks):
        for k in ks:
            pltpu.make_async_copy(tab_ref.at[idx_ref[t, k]], bufs[slot].at[:, k, :], sem_ref.at[slot]).start(priority=k % 2)

    def wait(slot):
        pltpu.make_async_copy(bufs[slot], bufs[slot], sem_ref.at[slot]).wait()

    diag = lax.broadcasted_iota(i32, (NE, LANES), 0) == lax.broadcasted_iota(i32, (NE, LANES), 1)

    def evaluate(t, slot, ahead):
        nslot = (slot + ahead_tokens) % PEER_SLOTS

        def step_done(i):
            if ahead:
                issue(t + ahead_tokens, nslot, range(i * per_step, (i + 1) * per_step))

        xrow = h_ref[pl.ds(t, 1), :]
        part = jnp.zeros((NE, LANES), f32)
        for c in range(n_lt):
            cols = slice(c * LANES, (c + 1) * LANES)
            part = part + pltpu.bitcast(bufs[slot][c] << 16, f32) * xrow[:, cols]
            step_done(c)
        g_col = jnp.sum(jnp.where(diag, g_ref[pl.ds(t, 1), :], 0.0), axis=1, keepdims=True)
        act = jnp.sum(part, axis=1, keepdims=True)
        w_col = jnp.broadcast_to(_gelu(act) * g_col, (NE, LANES))
        outs = []
        for c in range(n_lt):
            up = pltpu.bitcast(bufs[slot][c] & jnp.uint32(0xFFFF0000), f32)
            outs.append(jnp.sum(up * w_col, axis=0, keepdims=True))
            step_done(n_lt + c)
        y = jnp.concatenate(outs, axis=1)
        o_ref[pl.ds(t, 1), :] = x_ref[pl.ds(t, 1), :] + gt_ref[0] * y

    def group(t0, s0, ahead):
        for u in range(PEER_GROUP):
            wait(s0 + u)
        for u in range(PEER_GROUP):
            evaluate(t0 + u, s0 + u, ahead)

    def prime(_, carry):
        for t in range(ahead_tokens):
            issue(t, t, range(NE))
        return carry

    lax.fori_loop(0, jnp.where(step == 0, 1, 0), prime, 0)

    def steady(j, carry):
        for s0 in range(0, PEER_SLOTS, PEER_GROUP):
            group(j * PEER_SLOTS + s0, s0, True)
        return carry

    lax.fori_loop(0, tt // PEER_SLOTS - 1, steady, 0)
    for s0 in range(0, PEER_SLOTS, PEER_GROUP):
        group(tt - PEER_SLOTS + s0, s0, True)

    @pl.when(step == pl.num_programs(0) - 1)
    def _():
        for t in range(ahead_tokens):
            wait(t)


def _peer_ffn(eidx, gsm, h2, x1, gt2, table, seq, *, tt=128):
    T, D = h2.shape
    B = gt2.shape[0]
    NE = eidx.shape[1]
    ahead_tokens = PEER_GROUP * PEER_AHEAD
    assert tt % PEER_SLOTS == 0 and tt >= 2 * PEER_SLOTS and NE % (2 * D // LANES) == 0 and ahead_tokens <= SUBLANES
    per_b = seq // tt
    eidx_ext = jnp.concatenate([eidx, eidx[T - SUBLANES:]], axis=0)
    return pl.pallas_call(
        functools.partial(_peer_ffn_kernel, tt=tt),
        grid=(T // tt,),
        in_specs=[
            pl.BlockSpec((pl.Element(tt + SUBLANES), pl.Element(NE)), lambda i: (i * tt, 0), memory_space=pltpu.SMEM),
            pl.BlockSpec((tt, NE), lambda i: (i, 0)),
            pl.BlockSpec((tt, D), lambda i: (i, 0)),
            pl.BlockSpec((tt, D), lambda i: (i, 0)),
            pl.BlockSpec((1, 1, D), lambda i: (i // per_b, 0, 0)),
            pl.BlockSpec(memory_space=pl.ANY),
        ],
        out_specs=pl.BlockSpec((tt, D), lambda i: (i, 0)),
        out_shape=jax.ShapeDtypeStruct((T, D), f32),
        scratch_shapes=([pltpu.VMEM((D // LANES, NE, LANES), jnp.uint32)] * PEER_SLOTS
                        + [pltpu.SemaphoreType.DMA((PEER_SLOTS,))]),
        compiler_params=_cparams("arbitrary"),
        name="peer_ffn",
    )(eidx_ext, gsm, h2, x1, gt2.reshape(B, 1, D), table)


def _pack_kernel(d_ref, u_ref, o_ref):
    lo = pltpu.bitcast(d_ref[...].astype(bf16).astype(f32), jnp.uint32) >> 16
    hi = pltpu.bitcast(u_ref[...].astype(bf16).astype(f32), jnp.uint32)
    packed = lo | hi
    for c in range(o_ref.shape[1]):
        o_ref[:, c, :] = packed[:, c * LANES:(c + 1) * LANES]


def _pack_experts(down, up, *, tb=256):
    E, D = down.shape
    spec = pl.BlockSpec((tb, D), lambda i: (i, 0))
    return pl.pallas_call(
        _pack_kernel,
        grid=(E // tb,),
        in_specs=[spec, spec],
        out_specs=pl.BlockSpec((tb, D // LANES, LANES), lambda i: (i, 0, 0)),
        out_shape=jax.ShapeDtypeStruct((E, D // LANES, LANES), jnp.uint32),
        compiler_params=_cparams("arbitrary"),
        name="pack_experts",
    )(down, up)


def _rope_tables(S):
    half = HEAD_DIM // 2
    freqs = ROPE_THETA ** (-jnp.arange(half, dtype=f32) / half)
    ang = jnp.arange(S, dtype=f32)[:, None] * freqs[None, :]
    cos, sin = jnp.cos(ang), jnp.sin(ang)
    return jnp.concatenate([cos, cos], axis=1), jnp.concatenate([-sin, sin], axis=1)


def _layer(x, c, ada_w, ada_b, norm_mix_g, norm_ffn_g, w_in, w_out, q_norm_g, k_norm_g, cmp_pe_k, cmp_pe_v,
           cmp_w_k, cmp_w_v, gate_b, conv_w, conv_b, lru_wa, lru_ba, lru_wi, lru_bi, lru_lam, out_g_attn,
           out_g_rnn, peer_wq, peer_keys, peer_down, peer_up):
    B, S, D = x.shape
    T = B * S
    xf = x.reshape(T, D)
    mod = _mod(c, ada_w, ada_b)
    sh1, sc1, gt1, sh2, sc2, gt2 = [mod[:, k * D:(k + 1) * D] for k in range(6)]

    q0, g0 = NSA_WIDTH, NSA_WIDTH + 6 * KV_COLS
    r0 = g0 + GATE_COLS
    kvs = [w_in[:, q0 + k * KV_COLS:q0 + (k + 1) * KV_COLS] for k in range(6)]
    w_main = jnp.concatenate([w_in[:, :q0], w_in[:, r0:r0 + 2 * RNN_WIDTH]] + kvs, axis=1).astype(bf16)
    per_g = GATE_COLS // NSA_KV_HEADS
    wg = jnp.zeros((D, NSA_KV_HEADS * LANES), f32)
    gb = jnp.zeros((1, NSA_KV_HEADS * LANES), f32)
    for g in range(NSA_KV_HEADS):
        wg = wg.at[:, g * LANES:g * LANES + per_g].set(w_in[:, g0 + g * per_g:g0 + (g + 1) * per_g])
        gb = gb.at[0, g * LANES:g * LANES + per_g].set(gate_b[g * per_g:(g + 1) * per_g])
    z, gates = _norm_mm(xf, sc1, sh1, norm_mix_g, w_main, S, tm=1024, tn=768, wg=wg.astype(bf16), gb=gb)

    cos, sin = _rope_tables(S)
    q, kc_r, vc_r, ks, vs, kw, vw = _prep(z, cos, sin, q_norm_g, k_norm_g, B, S)
    kc, vc = _compress(kc_r, vc_r, cmp_pe_k, cmp_pe_v, cmp_w_k, cmp_w_v, k_norm_g[0])
    o_attn = _attention(q, kc, vc, ks, vs, kw, vw, gates)
    o_rnn = _rglru(z, conv_w, conv_b, lru_wa, lru_ba, lru_wi, lru_bi, lru_lam, B, S)
    x1 = _outproj(o_attn, o_rnn, out_g_attn, out_g_rnn, w_out.astype(bf16), xf, gt1, S)

    qp, h2 = _norm_mm(x1, sc2, sh2, norm_ffn_g, peer_wq.astype(bf16), S, tm=512, tn=2048, emit_h=True)
    eidx, gsm = _peer_topk(qp, peer_keys)
    out = _peer_ffn(eidx, gsm, h2, x1, gt2, _pack_experts(peer_down, peer_up), S)
    return out.reshape(B, S, D)


def kernel(x, c, ada_w, ada_b, norm_mix_g, norm_ffn_g, w_in, w_out, q_norm_g, k_norm_g, cmp_pe_k, cmp_pe_v, cmp_w_k, cmp_w_v, gate_b, conv_w, conv_b, lru_wa, lru_ba, lru_wi, lru_bi, lru_lam, out_g_attn, out_g_rnn, peer_wq, peer_keys, peer_down, peer_up):
    params = (ada_w, ada_b, norm_mix_g, norm_ffn_g, w_in, w_out, q_norm_g, k_norm_g, cmp_pe_k, cmp_pe_v, cmp_w_k,
              cmp_w_v, gate_b, conv_w, conv_b, lru_wa, lru_ba, lru_wi, lru_bi, lru_lam, out_g_attn, out_g_rnn,
              peer_wq, peer_keys, peer_down, peer_up)
    for l in range(ada_w.shape[0]):
        x = _layer(x, c, *[p[l] for p in params])
    return x
```

```python
import functools

import numpy as np
import jax
import jax.numpy as jnp
from jax import lax
from jax.experimental import pallas as pl
from jax.experimental.pallas import tpu as pltpu

f32 = jnp.float32
bf16 = jnp.bfloat16
i32 = jnp.int32

HEAD_DIM = 128
NSA_HEADS = 8
NSA_KV_HEADS = 2
NSA_GROUP = NSA_HEADS // NSA_KV_HEADS
NSA_WIDTH = NSA_HEADS * HEAD_DIM
KV_COLS = NSA_KV_HEADS * HEAD_DIM
GATE_COLS = 3 * NSA_HEADS
CMP_LEN = 32
CMP_STRIDE = 16
SLC_LEN = 64
N_SELECT = 16
FORCED_SCORE = 1e4
WINDOW = 512
RNN_WIDTH = 1024
RNN_BLOCKS = 8
RNN_BLOCK_DIM = RNN_WIDTH // RNN_BLOCKS
CONV_WIDTH = 4
LRU_C = 8.0
PEER_HEADS = 8
PEER_KEYS = 128
PEER_TOPK = 16
ROPE_THETA = 10000.0
EPS = 1e-6
NEG_BIG = -1e30

LANES = 128
SUBLANES = 8
VMEM_LIMIT = 56 * 1024 * 1024

NT_DIMS = (((1,), (1,)), ((), ()))


def _cparams(*sem):
    return pltpu.CompilerParams(dimension_semantics=sem, vmem_limit_bytes=VMEM_LIMIT)


def _rms(x, g):
    return x * lax.rsqrt(jnp.mean(x * x, axis=-1, keepdims=True) + EPS) * g


def _mod_kernel(c_ref, w_ref, b_ref, o_ref):
    c = c_ref[...]
    a = c * jax.nn.sigmoid(c)
    o_ref[...] = jnp.dot(a, w_ref[...], preferred_element_type=f32) + b_ref[...]


def _mod(c, w, b, tn=1024):
    B, D = c.shape
    N = w.shape[1]
    return pl.pallas_call(
        _mod_kernel,
        grid=(N // tn,),
        in_specs=[
            pl.BlockSpec((B, D), lambda j: (0, 0)),
            pl.BlockSpec((D, tn), lambda j: (0, j)),
            pl.BlockSpec((1, tn), lambda j: (0, j)),
        ],
        out_specs=pl.BlockSpec((B, tn), lambda j: (0, j)),
        out_shape=jax.ShapeDtypeStruct((B, N), f32),
        compiler_params=_cparams("arbitrary"),
        name="mod",
    )(c, w, b.reshape(1, N))


def _norm_mm_kernel(x_ref, sc_ref, sh_ref, g_ref, w_ref, *rest, n_extra, emit_h):
    rest = list(rest)
    if n_extra:
        wg_ref, gb_ref = rest.pop(0), rest.pop(0)
    z_ref = rest.pop(0)
    if n_extra:
        gate_ref = rest.pop(0)
    if emit_h:
        h_ref = rest.pop(0)
    hb_ref = rest.pop(0)

    @pl.when(pl.program_id(1) == 0)
    def _():
        h = _rms(x_ref[...], g_ref[...]) * (1.0 + sc_ref[0]) + sh_ref[0]
        hb = h.astype(bf16)
        hb_ref[...] = hb
        if emit_h:
            h_ref[...] = h
        if n_extra:
            gl = jnp.dot(hb, wg_ref[...], preferred_element_type=f32)
            gate_ref[...] = jax.nn.sigmoid(gl + gb_ref[...])

    z_ref[...] = jnp.dot(hb_ref[...], w_ref[...], preferred_element_type=f32)


def _norm_mm(x, sc, sh, g, w, seq, *, tm, tn, wg=None, gb=None, emit_h=False):
    T, D = x.shape
    N = w.shape[1]
    B = sc.shape[0]
    assert seq % tm == 0 and N % tn == 0
    per_b = seq // tm
    n_extra = 0 if wg is None else wg.shape[1]
    in_specs = [
        pl.BlockSpec((tm, D), lambda i, j: (i, 0)),
        pl.BlockSpec((1, 1, D), lambda i, j: (i // per_b, 0, 0)),
        pl.BlockSpec((1, 1, D), lambda i, j: (i // per_b, 0, 0)),
        pl.BlockSpec((1, D), lambda i, j: (0, 0)),
        pl.BlockSpec((D, tn), lambda i, j: (0, j)),
    ]
    args = [x, sc.reshape(B, 1, D), sh.reshape(B, 1, D), g.reshape(1, D), w]
    out_specs = [pl.BlockSpec((tm, tn), lambda i, j: (i, j))]
    out_shape = [jax.ShapeDtypeStruct((T, N), f32)]
    if n_extra:
        in_specs += [pl.BlockSpec((D, n_extra), lambda i, j: (0, 0)), pl.BlockSpec((1, n_extra), lambda i, j: (0, 0))]
        args += [wg, gb]
        out_specs.append(pl.BlockSpec((tm, n_extra), lambda i, j: (i, 0)))
        out_shape.append(jax.ShapeDtypeStruct((T, n_extra), f32))
    if emit_h:
        out_specs.append(pl.BlockSpec((tm, D), lambda i, j: (i, 0)))
        out_shape.append(jax.ShapeDtypeStruct((T, D), f32))
    return pl.pallas_call(
        functools.partial(_norm_mm_kernel, n_extra=n_extra, emit_h=emit_h),
        grid=(T // tm, N // tn),
        in_specs=in_specs,
        out_specs=out_specs,
        out_shape=out_shape,
        scratch_shapes=[pltpu.VMEM((tm, D), bf16)],
        compiler_params=_cparams("arbitrary", "arbitrary"),
        name="norm_mm",
    )(*args)


def _prep_kernel(zq_ref, zc_ref, zs_ref, zw_ref, cos_ref, sin_ref, qg_ref, kg_ref,
                 q_ref, kc_ref, vc_ref, ks_ref, vs_ref, kw_ref, vw_ref):
    cos = cos_ref[...]
    sin = sin_ref[...]

    def rope(y):
        return y * cos + pltpu.roll(y, HEAD_DIM // 2, 1) * sin

    def head(ref, h):
        return ref[:, h * HEAD_DIM:(h + 1) * HEAD_DIM]

    qg = qg_ref[...]
    for h in range(NSA_HEADS):
        q_ref[0, h] = rope(_rms(head(zq_ref, h), qg)).astype(bf16)
    for g in range(NSA_KV_HEADS):
        kc_ref[0, g] = rope(head(zc_ref, g))
        vc_ref[0, g] = head(zc_ref, NSA_KV_HEADS + g)
        ks_ref[0, g] = rope(_rms(head(zs_ref, g), kg_ref[1:2])).astype(bf16)
        vs_ref[0, g] = head(zs_ref, NSA_KV_HEADS + g).astype(bf16)
        kw_ref[0, g] = rope(_rms(head(zw_ref, g), kg_ref[2:3])).astype(bf16)
        vw_ref[0, g] = head(zw_ref, NSA_KV_HEADS + g).astype(bf16)


def _prep(z, cos, sin, q_g, k_g, B, S, *, ts=256):
    per_b = S // ts
    G = NSA_KV_HEADS
    kv = 2 * KV_COLS
    base = (NSA_WIDTH + 2 * RNN_WIDTH) // kv

    def zspec(width, idx):
        return pl.BlockSpec((ts, width), lambda b, i: (b * per_b + i, idx))

    def hm(nh, dt):
        return pl.BlockSpec((1, nh, ts, HEAD_DIM), lambda b, i: (b, 0, i, 0)), jax.ShapeDtypeStruct((B, nh, S, HEAD_DIM), dt)

    outs = [hm(NSA_HEADS, bf16), hm(G, f32), hm(G, f32), hm(G, bf16), hm(G, bf16), hm(G, bf16), hm(G, bf16)]
    return pl.pallas_call(
        _prep_kernel,
        grid=(B, per_b),
        in_specs=[
            zspec(NSA_WIDTH, 0), zspec(kv, base), zspec(kv, base + 1), zspec(kv, base + 2),
            pl.BlockSpec((ts, HEAD_DIM), lambda b, i: (i, 0)),
            pl.BlockSpec((ts, HEAD_DIM), lambda b, i: (i, 0)),
            pl.BlockSpec((1, HEAD_DIM), lambda b, i: (0, 0)),
            pl.BlockSpec((3, HEAD_DIM), lambda b, i: (0, 0)),
        ],
        out_specs=[o[0] for o in outs],
        out_shape=[o[1] for o in outs],
        compiler_params=_cparams("arbitrary", "arbitrary"),
        name="prep",
    )(z, z, z, z, cos, sin, q_g.reshape(1, HEAD_DIM), k_g)


def _compress_kernel(kseg_ref, vseg_ref, pek_ref, pev_ref, wk_ref, wv_ref, kg_ref, kc_ref, vc_ref, *, n_cmp):
    n_seg = kseg_ref.shape[2]
    half = CMP_STRIDE * HEAD_DIM
    row = lax.broadcasted_iota(i32, (n_seg, HEAD_DIM), 0)

    def comp(seg_ref, pe_ref, w_ref):
        seg = seg_ref[0, 0]
        lo = jnp.dot((seg + pe_ref[0:1]).astype(bf16), w_ref[0:half], preferred_element_type=f32)
        hi = jnp.dot((seg + pe_ref[1:2]).astype(bf16), w_ref[half:2 * half], preferred_element_type=f32)
        return lo + pltpu.roll(hi, n_seg - 1, 0)

    kc = _rms(comp(kseg_ref, pek_ref, wk_ref), kg_ref[...])
    vc = comp(vseg_ref, pev_ref, wv_ref)
    kc_ref[0, 0] = jnp.where(row < n_cmp, kc, 0.0).astype(bf16)
    vc_ref[0, 0] = jnp.where(row < n_cmp, vc, 0.0).astype(bf16)


def _compress(kc_r, vc_r, pe_k, pe_v, w_k, w_v, k_g0):
    B, G, S, dh = kc_r.shape
    n_seg = S // CMP_STRIDE
    n_cmp = (S - CMP_LEN) // CMP_STRIDE + 1
    half = CMP_STRIDE * dh
    seg = lambda a: a.reshape(B, G, n_seg, half)
    segspec = pl.BlockSpec((1, 1, n_seg, half), lambda b, g: (b, g, 0, 0))
    full = lambda shape: pl.BlockSpec(shape, lambda b, g: (0,) * len(shape))
    ospec = pl.BlockSpec((1, 1, n_seg, dh), lambda b, g: (b, g, 0, 0))
    return pl.pallas_call(
        functools.partial(_compress_kernel, n_cmp=n_cmp),
        grid=(B, G),
        in_specs=[segspec, segspec, full((2, half)), full((2, half)), full((2 * half, dh)), full((2 * half, dh)),
                  full((1, dh))],
        out_specs=[ospec, ospec],
        out_shape=[jax.ShapeDtypeStruct((B, G, n_seg, dh), bf16)] * 2,
        compiler_params=_cparams("arbitrary", "arbitrary"),
        name="compress",
    )(seg(kc_r), seg(vc_r), pe_k.reshape(2, half), pe_v.reshape(2, half), w_k.astype(bf16), w_v.astype(bf16),
      k_g0.reshape(1, dh))


def _attn_kernel(q_ref, kc_ref, vc_ref, ks_ref, vs_ref, kw_ref, vw_ref, gate_ref, ovt_ref, ex_ref, cb_ref, wb_ref,
                 o_ref, *, tq, tk, n_cmp, n_slc, n_sel):
    R = NSA_GROUP
    scale = HEAD_DIM ** -0.5
    exp2_scale = scale * 1.4426950408889634
    step = pl.program_id(2)
    t0 = step * tq
    q = q_ref[0].reshape(R * tq, HEAD_DIM)
    row_t4 = t0 + (lax.broadcasted_iota(i32, (R * tq, 1), 0) & (tq - 1))
    per_head = lambda a: jnp.concatenate([a] * R, axis=0)

    ncl = kc_ref.shape[2]
    cl = lax.broadcasted_iota(i32, (1, ncl), 1)
    cmp_end = jnp.where(cl < n_cmp, cl * CMP_STRIDE + (CMP_LEN - 1), jnp.int32(2 ** 30))
    s = lax.dot_general(q, kc_ref[0, 0], NT_DIMS, preferred_element_type=f32) * scale
    s = jnp.where(cmp_end <= row_t4, s, -jnp.inf)
    m = jnp.max(s, axis=-1, keepdims=True)
    m = jnp.where(m == -jnp.inf, 0.0, m)
    p = jnp.exp(s - m)
    d = jnp.sum(p, axis=-1, keepdims=True)
    p = p / jnp.where(d > 0, d, 1.0)
    o_cmp = jnp.dot(p.astype(bf16), vc_ref[0, 0], preferred_element_type=f32)

    psum = p[0:tq]
    for r in range(1, R):
        psum = psum + p[r * tq:(r + 1) * tq]
    p_hi = psum.astype(bf16)
    p_lo = (psum - p_hi.astype(f32)).astype(bf16)
    ovt = ovt_ref[...]
    imp = (lax.dot_general(ovt, p_hi, NT_DIMS, preferred_element_type=f32)
           + lax.dot_general(ovt, p_lo, NT_DIMS, preferred_element_type=f32))
    nb = ovt.shape[0]
    jb = lax.broadcasted_iota(i32, (nb, tq), 0)
    tcol = t0 + lax.broadcasted_iota(i32, (nb, tq), 1)
    cur = tcol // SLC_LEN
    imp = jnp.where(jb * SLC_LEN <= tcol, imp, -FORCED_SCORE)
    imp = jnp.where(jb == 0, FORCED_SCORE, jnp.where(jb == cur, FORCED_SCORE, jnp.where(jb == cur - 1, FORCED_SCORE, imp)))
    imp = jnp.where(jb < n_slc, imp, -jnp.inf)
    rank = jnp.zeros((nb, tq), f32)
    for i in range(n_slc):
        row = imp[i:i + 1, :]
        later = jnp.where(jb > i, 1.0, 0.0)
        rank = rank + jnp.where(row > imp, 1.0, jnp.where(row == imp, later, 0.0))
    sel_bias = jnp.where(rank < n_sel, jnp.where(jb < n_slc, 0.0, NEG_BIG), NEG_BIG)
    sel_bias = jnp.concatenate([sel_bias, jnp.full((LANES - nb, tq), NEG_BIG, f32)], axis=0).T.astype(bf16)

    def slc_chunk(c, carry, causal_bias):
        m_i, l_i, acc = carry
        k0 = pl.multiple_of(c * tk, tk)
        k = ks_ref[0, 0, pl.ds(k0, tk), :]
        v = vs_ref[0, 0, pl.ds(k0, tk), :]
        bias = jnp.dot(sel_bias, ex_ref[c], preferred_element_type=f32)
        if causal_bias is not None:
            bias = bias + causal_bias
        sc = lax.dot_general(q, k, NT_DIMS, preferred_element_type=f32) + per_head(bias)
        m_n = jnp.maximum(m_i, jnp.max(sc, axis=-1, keepdims=True))
        alpha = jnp.exp2((m_i - m_n) * exp2_scale)
        pc = jnp.exp2((sc - m_n) * exp2_scale)
        l_n = alpha * l_i + jnp.sum(pc, axis=-1, keepdims=True)
        acc_n = alpha * acc + jnp.dot(pc.astype(bf16), v, preferred_element_type=f32)
        return m_n, l_n, acc_n

    last = (t0 + tq - 1) // tk
    init = (jnp.full((R * tq, 1), NEG_BIG, f32), jnp.zeros((R * tq, 1), f32), jnp.zeros((R * tq, HEAD_DIM), f32))
    carry = lax.fori_loop(0, last, lambda c, cr: slc_chunk(c, cr, None), init)
    _, l_s, acc_s = slc_chunk(last, carry, cb_ref[(t0 - last * tk) // tq])
    o_slc = acc_s / l_s

    wk = WINDOW + tq
    w0 = pl.multiple_of(jnp.maximum(t0 - WINDOW, 0), tq)
    sw = lax.dot_general(q, kw_ref[0, 0, pl.ds(w0, wk), :], NT_DIMS, preferred_element_type=f32)
    sw = sw + per_head(wb_ref[jnp.minimum(step, WINDOW // tq)])
    pw = jnp.exp2((sw - jnp.max(sw, axis=-1, keepdims=True)) * exp2_scale)
    o_win = jnp.dot(pw.astype(bf16), vw_ref[0, 0, pl.ds(w0, wk), :], preferred_element_type=f32)
    o_win = o_win / jnp.sum(pw, axis=-1, keepdims=True)

    gates = gate_ref[...]
    for r in range(R):
        rows = slice(r * tq, (r + 1) * tq)
        o_ref[:, r * HEAD_DIM:(r + 1) * HEAD_DIM] = (
            gates[:, 3 * r:3 * r + 1] * o_cmp[rows]
            + gates[:, 3 * r + 1:3 * r + 2] * o_slc[rows]
            + gates[:, 3 * r + 2:3 * r + 3] * o_win[rows])


def _attention(q, kc, vc, ks, vs, kw, vw, gates, *, tq=128, tk=512):
    B, H, S, dh = q.shape
    G = NSA_KV_HEADS
    n_seg = kc.shape[2]
    n_cmp = (S - CMP_LEN) // CMP_STRIDE + 1
    n_slc = S // SLC_LEN
    n_sel = min(N_SELECT, n_slc)
    assert n_seg == LANES and n_slc <= LANES and S % tk == 0 and tk % tq == 0 and S >= WINDOW + tq
    per_b = S // tq
    cs = np.arange(n_seg)[:, None] * CMP_STRIDE
    ss = np.arange(LANES)[None, :] * SLC_LEN
    ov = np.maximum(np.minimum(cs + CMP_LEN, ss + SLC_LEN) - np.maximum(cs, ss), 0).astype(np.float32) / CMP_LEN
    ov[n_cmp:, :] = 0.0
    ov[:, n_slc:] = 0.0
    nb = -(-n_slc // SUBLANES) * SUBLANES
    ovt = np.ascontiguousarray(ov.T[:nb])
    kpos = np.arange(S).reshape(S // tk, 1, tk)
    ex = (kpos // SLC_LEN == np.arange(LANES)[None, :, None]).astype(np.float32)
    r_ = np.arange(tq)[None, :, None]
    cb = np.where(np.arange(tk)[None, None, :] <= np.arange(tk // tq)[:, None, None] * tq + r_, 0.0, NEG_BIG)
    wk = WINDOW + tq
    col = np.arange(wk)[None, None, :]
    early = col <= np.arange(WINDOW // tq)[:, None, None] * tq + r_
    full = (col <= r_ + WINDOW) & (col > r_)
    wb = np.where(np.concatenate([early, full], axis=0), 0.0, NEG_BIG)
    const = lambda a: pl.BlockSpec(a.shape, lambda b, g, i: (0,) * a.ndim)
    kvspec = pl.BlockSpec((1, 1, S, dh), lambda b, g, i: (b, g, 0, 0))
    cspec = pl.BlockSpec((1, 1, n_seg, dh), lambda b, g, i: (b, g, 0, 0))
    return pl.pallas_call(
        functools.partial(_attn_kernel, tq=tq, tk=tk, n_cmp=n_cmp, n_slc=n_slc, n_sel=n_sel),
        grid=(B, G, per_b),
        in_specs=[
            pl.BlockSpec((1, NSA_GROUP, tq, dh), lambda b, g, i: (b, g, i, 0)),
            cspec, cspec, kvspec, kvspec, kvspec, kvspec,
            pl.BlockSpec((tq, LANES), lambda b, g, i: (b * per_b + i, g)),
            const(ovt), const(ex), const(cb), const(wb),
        ],
        out_specs=pl.BlockSpec((tq, NSA_GROUP * dh), lambda b, g, i: (b * per_b + i, g)),
        out_shape=jax.ShapeDtypeStruct((B * S, NSA_WIDTH), f32),
        compiler_params=_cparams("arbitrary", "arbitrary", "arbitrary"),
        name="nsa_attention",
    )(q, kc, vc, ks, vs, kw, vw, gates, jnp.asarray(ovt, bf16), jnp.asarray(ex, bf16), jnp.asarray(cb, f32),
      jnp.asarray(wb, f32))


def _gelu(x):
    return 0.5 * x * (1.0 + lax.erf(x * (2.0 ** -0.5)))


def _rglru_kernel(xr_ref, xg_ref, cw_ref, cb_ref, wa_ref, ba_ref, wi_ref, bi_ref, lam_ref, o_ref,
                  ext_ref, a_ref, b_ref, hs_ref, hc_ref, *, ts):
    halo = 8

    @pl.when(pl.program_id(1) == 0)
    def _():
        ext_ref[0:halo] = jnp.zeros((halo, RNN_WIDTH), f32)
        hc_ref[...] = jnp.zeros_like(hc_ref)

    xr = xr_ref[...]
    ext_ref[halo:halo + ts] = xr
    u = cb_ref[...] + cw_ref[CONV_WIDTH - 1:CONV_WIDTH] * xr
    for k in range(1, CONV_WIDTH):
        u = u + cw_ref[CONV_WIDTH - 1 - k:CONV_WIDTH - k] * ext_ref[halo - k:halo - k + ts]
    ext_ref[0:halo] = xr[ts - halo:ts]

    ra, ri = [], []
    for n in range(RNN_BLOCKS):
        ub = u[:, n * RNN_BLOCK_DIM:(n + 1) * RNN_BLOCK_DIM].astype(bf16)
        ra.append(jnp.dot(ub, wa_ref[n], preferred_element_type=f32))
        ri.append(jnp.dot(ub, wi_ref[n], preferred_element_type=f32))
    r = jax.nn.sigmoid(jnp.concatenate(ra, axis=1) + ba_ref[...])
    ig = jax.nn.sigmoid(jnp.concatenate(ri, axis=1) + bi_ref[...])
    log_a = (-LRU_C) * r * jax.nn.softplus(-lam_ref[...])
    a_ref[...] = jnp.exp(log_a)
    b_ref[...] = jnp.sqrt(1.0 - jnp.exp(2.0 * log_a)) * ig * u

    sub = lax.broadcasted_iota(i32, (8, RNN_WIDTH), 0)

    def blk(j, h):
        r0 = pl.multiple_of(j * 8, 8)
        a8 = a_ref[pl.ds(r0, 8), :]
        b8 = b_ref[pl.ds(r0, 8), :]
        out = jnp.zeros((8, RNN_WIDTH), f32)
        for i in range(8):
            h = a8[i:i + 1] * h + b8[i:i + 1]
            out = jnp.where(sub == i, h, out)
        hs_ref[pl.ds(r0, 8), :] = out
        return h

    hc_ref[...] = lax.fori_loop(0, ts // 8, blk, hc_ref[...])
    o_ref[...] = _gelu(xg_ref[...]) * hs_ref[...]


def _rglru(z, conv_w, conv_b, wa, ba, wi, bi, lam, B, S, *, ts=512):
    W = RNN_WIDTH
    per_b = S // ts
    xr_idx = NSA_WIDTH // W
    vec = lambda a: a.reshape(1, W)
    full = lambda shape: pl.BlockSpec(shape, lambda b, i: (0,) * len(shape))
    return pl.pallas_call(
        functools.partial(_rglru_kernel, ts=ts),
        grid=(B, per_b),
        in_specs=[
            pl.BlockSpec((ts, W), lambda b, i: (b * per_b + i, xr_idx)),
            pl.BlockSpec((ts, W), lambda b, i: (b * per_b + i, xr_idx + 1)),
            full((CONV_WIDTH, W)), full((1, W)),
            full((RNN_BLOCKS, RNN_BLOCK_DIM, RNN_BLOCK_DIM)), full((1, W)),
            full((RNN_BLOCKS, RNN_BLOCK_DIM, RNN_BLOCK_DIM)), full((1, W)), full((1, W)),
        ],
        out_specs=pl.BlockSpec((ts, W), lambda b, i: (b * per_b + i, 0)),
        out_shape=jax.ShapeDtypeStruct((B * S, W), f32),
        scratch_shapes=[pltpu.VMEM((ts + 8, W), f32), pltpu.VMEM((ts, W), f32), pltpu.VMEM((ts, W), f32),
                        pltpu.VMEM((ts, W), f32), pltpu.VMEM((1, W), f32)],
        compiler_params=_cparams("arbitrary", "arbitrary"),
        name="rglru",
    )(z, z, conv_w, vec(conv_b), wa.astype(bf16), vec(ba), wi.astype(bf16), vec(bi), vec(lam))


def _outproj_kernel(oa_ref, or_ref, ga_ref, gr_ref, w_ref, x_ref, gt_ref, o_ref, yb_ref):
    @pl.when(pl.program_id(1) == 0)
    def _():
        wa = oa_ref.shape[1]
        yb_ref[:, 0:wa] = _rms(oa_ref[...], ga_ref[...]).astype(bf16)
        yb_ref[:, wa:] = _rms(or_ref[...], gr_ref[...]).astype(bf16)

    o_ref[...] = x_ref[...] + gt_ref[0] * jnp.dot(yb_ref[...], w_ref[...], preferred_element_type=f32)


def _outproj(o_attn, o_rnn, g_a, g_r, w, x, gt, seq, *, tm=512, tn=2048):
    T, D = x.shape
    B = gt.shape[0]
    wa, wr = o_attn.shape[1], o_rnn.shape[1]
    per_b = seq // tm
    return pl.pallas_call(
        _outproj_kernel,
        grid=(T // tm, D // tn),
        in_specs=[
            pl.BlockSpec((tm, wa), lambda i, j: (i, 0)),
            pl.BlockSpec((tm, wr), lambda i, j: (i, 0)),
            pl.BlockSpec((1, wa), lambda i, j: (0, 0)),
            pl.BlockSpec((1, wr), lambda i, j: (0, 0)),
            pl.BlockSpec((wa + wr, tn), lambda i, j: (0, j)),
            pl.BlockSpec((tm, tn), lambda i, j: (i, j)),
            pl.BlockSpec((1, 1, tn), lambda i, j: (i // per_b, 0, j)),
        ],
        out_specs=pl.BlockSpec((tm, tn), lambda i, j: (i, j)),
        out_shape=jax.ShapeDtypeStruct((T, D), f32),
        scratch_shapes=[pltpu.VMEM((tm, wa + wr), bf16)],
        compiler_params=_cparams("arbitrary", "arbitrary"),
        name="out_proj",
    )(o_attn, o_rnn, g_a.reshape(1, wa), g_r.reshape(1, wr), w, x, gt.reshape(B, 1, D))


def _top_rows(s, k):
    n = s.shape[0]
    io = lax.broadcasted_iota(i32, s.shape, 0)
    vals, idxs = [], []
    for _ in range(k):
        m = jnp.max(s, axis=0, keepdims=True)
        ix = jnp.min(jnp.where(s == m, io, n), axis=0, keepdims=True)
        vals.append(m)
        idxs.append(ix)
        s = jnp.where(io == ix, -jnp.inf, s)
    return vals, idxs


def _peer_topk_kernel(q_ref, keys_ref, e_ref, g_ref):
    K = PEER_TOPK
    tt = q_ref.shape[0]
    pairs = [(a, b) for a in range(K) for b in range(K) if (a + 1) * (b + 1) <= K]
    n_cand = len(pairs) + (-len(pairs) % 8)
    cio = lax.broadcasted_iota(i32, (n_cand, tt), 0)
    oio = lax.broadcasted_iota(i32, (PEER_HEADS * K, tt), 0)
    e_all = jnp.zeros((PEER_HEADS * K, tt), i32)
    g_all = jnp.zeros((PEER_HEADS * K, tt), f32)
    for h in range(PEER_HEADS):
        tops = []
        for p in range(2):
            c0 = (2 * h + p) * PEER_KEYS
            qhp = q_ref[:, c0:c0 + PEER_KEYS].astype(bf16)
            s = lax.dot_general(keys_ref[h, p], qhp, NT_DIMS, preferred_element_type=f32)
            tops.append(_top_rows(s, K))
        (v1, i1), (v2, i2) = tops
        cand = jnp.full((n_cand, tt), -jnp.inf, f32)
        cidx = jnp.zeros((n_cand, tt), i32)
        for r, (a, b) in enumerate(pairs):
            cand = jnp.where(cio == r, v1[a] + v2[b], cand)
            cidx = jnp.where(cio == r, i1[a] * PEER_KEYS + i2[b], cidx)
        vals, pos = _top_rows(cand, K)
        ex = [jnp.exp(v - vals[0]) for v in vals]
        den = functools.reduce(lambda x, y: x + y, ex)
        for r in range(K):
            e_r = jnp.sum(jnp.where(cio == pos[r], cidx, 0), axis=0, keepdims=True)
            e_all = jnp.where(oio == h * K + r, e_r, e_all)
            g_all = jnp.where(oio == h * K + r, ex[r] / den, g_all)
    e_ref[...] = e_all.T
    g_ref[...] = g_all.T


def _peer_topk(qp, keys, *, tt=256):
    T, N = qp.shape
    HK = PEER_HEADS * PEER_TOPK
    return pl.pallas_call(
        _peer_topk_kernel,
        grid=(T // tt,),
        in_specs=[pl.BlockSpec((tt, N), lambda i: (i, 0)),
                  pl.BlockSpec(keys.shape, lambda i: (0, 0, 0, 0))],
        out_specs=[pl.BlockSpec((tt, HK), lambda i: (i, 0))] * 2,
        out_shape=[jax.ShapeDtypeStruct((T, HK), i32), jax.ShapeDtypeStruct((T, HK), f32)],
        compiler_params=_cparams("arbitrary"),
        name="peer_topk",
    )(qp, keys.astype(bf16))


PEER_GROUP = 2
PEER_AHEAD = 3
PEER_SLOTS = PEER_GROUP * (PEER_AHEAD + 1)


def _sublane_sums(qs, sub):
    order = [0, 4, 2, 6, 1, 5, 3, 7]
    level = [qs[i] for i in order]
    for dist in (4, 2, 1):
        first = (sub & dist) == 0
        nxt = []
        for a, b in zip(level[0::2], level[1::2]):
            nxt.append(jnp.where(first, a, pltpu.roll(b, dist, 0)) + jnp.where(first, pltpu.roll(a, SUBLANES - dist, 0), b))
        level = nxt
    return level[0]


def _peer_ffn_kernel(idx_ref, g_ref, h_ref, x_ref, gt_ref, tab_ref, o_ref, *scratch, tt):
    bufs, sem_ref = scratch[:PEER_SLOTS], scratch[PEER_SLOTS]
    NE = idx_ref.shape[1]
    n_lt = h_ref.shape[1] // LANES
    n_grp = NE // SUBLANES
    per_step = NE // (2 * n_grp)
    ahead_tokens = PEER_GROUP * PEER_AHEAD
    step = pl.program_id(0)
    assert n_lt == 2 * SUBLANES

    def issue(t, slot, ks):
        for k in ks:
            pltpu.make_async_copy(tab_ref.at[idx_ref[t, k]], bufs[slot].at[k], sem_ref.at[slot]).start(priority=k % 2)

    def wait(slot):
        pltpu.make_async_copy(bufs[slot], bufs[slot], sem_ref.at[slot]).wait()

    diag = lax.broadcasted_iota(i32, (NE, LANES), 0) == lax.broadcasted_iota(i32, (NE, LANES), 1)
    sub = lax.broadcasted_iota(i32, (SUBLANES, LANES), 0)

    def evaluate(t, slot, ahead):
        nslot = (slot + ahead_tokens) % PEER_SLOTS
        buf = bufs[slot]

        def step_done(i):
            if ahead:
                issue(t + ahead_tokens, nslot, range(i * per_step, (i + 1) * per_step))

        xrow = h_ref[pl.ds(t, 1), :]
        xa = jnp.zeros((SUBLANES, LANES), f32)
        xb = jnp.zeros((SUBLANES, LANES), f32)
        for c in range(SUBLANES):
            xa = jnp.where(sub == c, xrow[:, c * LANES:(c + 1) * LANES], xa)
            xb = jnp.where(sub == c, xrow[:, (SUBLANES + c) * LANES:(SUBLANES + c + 1) * LANES], xb)
        parts = []
        for g in range(n_grp):
            qs = []
            for j in range(SUBLANES):
                k = g * SUBLANES + j
                dk = buf[k, 0:n_lt, :].astype(f32)
                qs.append(dk[0:SUBLANES] * xa + dk[SUBLANES:] * xb)
            parts.append(_sublane_sums(qs, sub))
            step_done(g)
        part = jnp.concatenate(parts, axis=0)
        g_col = jnp.sum(jnp.where(diag, g_ref[pl.ds(t, 1), :], 0.0), axis=1, keepdims=True)
        act = jnp.sum(part, axis=1, keepdims=True)
        w_col = jnp.broadcast_to(_gelu(act) * g_col, (NE, LANES))
        acc_a = [jnp.zeros((SUBLANES, LANES), f32) for _ in range(4)]
        acc_b = [jnp.zeros((SUBLANES, LANES), f32) for _ in range(4)]
        for g in range(n_grp):
            for j in range(SUBLANES):
                k = g * SUBLANES + j
                w = jnp.broadcast_to(w_col[k:k + 1, :], (SUBLANES, LANES))
                uk = buf[k, n_lt:, :].astype(f32)
                acc_a[j % 4] = acc_a[j % 4] + uk[0:SUBLANES] * w
                acc_b[j % 4] = acc_b[j % 4] + uk[SUBLANES:] * w
            step_done(n_grp + g)
        ya = (acc_a[0] + acc_a[1]) + (acc_a[2] + acc_a[3])
        yb = (acc_b[0] + acc_b[1]) + (acc_b[2] + acc_b[3])
        y = jnp.concatenate([half[c:c + 1, :] for half in (ya, yb) for c in range(SUBLANES)], axis=1)
        o_ref[pl.ds(t, 1), :] = x_ref[pl.ds(t, 1), :] + gt_ref[0] * y

    def group(t0, s0, ahead):
        for u in range(PEER_GROUP):
            wait(s0 + u)
        for u in range(PEER_GROUP):
            evaluate(t0 + u, s0 + u, ahead)

    def prime(_, carry):
        for t in range(ahead_tokens):
            issue(t, t, range(NE))
        return carry

    lax.fori_loop(0, jnp.where(step == 0, 1, 0), prime, 0)

    def steady(j, carry):
        for s0 in range(0, PEER_SLOTS, PEER_GROUP):
            group(j * PEER_SLOTS + s0, s0, True)
        return carry

    lax.fori_loop(0, tt // PEER_SLOTS, steady, 0)

    @pl.when(step == pl.num_programs(0) - 1)
    def _():
        for t in range(ahead_tokens):
            wait(t)


def _peer_ffn(eidx, gsm, h2, x1, gt2, table, seq, *, tt=128):
    T, D = h2.shape
    B = gt2.shape[0]
    NE = eidx.shape[1]
    ahead_tokens = PEER_GROUP * PEER_AHEAD
    n_lt = D // LANES
    assert tt % PEER_SLOTS == 0 and tt >= 2 * PEER_SLOTS and NE % SUBLANES == 0 and ahead_tokens <= SUBLANES
    per_b = seq // tt
    eidx_ext = jnp.concatenate([eidx, eidx[T - SUBLANES:]], axis=0)
    tokens = pl.BlockSpec((tt, D), lambda i: (i, 0))
    return pl.pallas_call(
        functools.partial(_peer_ffn_kernel, tt=tt),
        grid=(T // tt,),
        in_specs=[
            pl.BlockSpec((pl.Element(tt + SUBLANES), pl.Element(NE)), lambda i: (i * tt, 0), memory_space=pltpu.SMEM),
            pl.BlockSpec((tt, NE), lambda i: (i, 0)),
            tokens,
            tokens,
            pl.BlockSpec((1, 1, D), lambda i: (i // per_b, 0, 0)),
            pl.BlockSpec(memory_space=pl.ANY),
        ],
        out_specs=tokens,
        out_shape=jax.ShapeDtypeStruct((T, D), f32),
        scratch_shapes=([pltpu.VMEM((NE, 2 * n_lt, LANES), bf16)] * PEER_SLOTS
                        + [pltpu.SemaphoreType.DMA((PEER_SLOTS,))]),
        compiler_params=_cparams("arbitrary"),
        name="peer_ffn",
    )(eidx_ext, gsm, h2, x1, gt2.reshape(B, 1, D), table)


def _pack_experts(down, up):
    E, D = down.shape
    rec = lambda a: a.astype(bf16).reshape(E, D // LANES, LANES)
    return jnp.concatenate([rec(down), rec(up)], axis=1)


def _rope_tables(S):
    half = HEAD_DIM // 2
    freqs = ROPE_THETA ** (-jnp.arange(half, dtype=f32) / half)
    ang = jnp.arange(S, dtype=f32)[:, None] * freqs[None, :]
    cos, sin = jnp.cos(ang), jnp.sin(ang)
    return jnp.concatenate([cos, cos], axis=1), jnp.concatenate([-sin, sin], axis=1)


def _layer(x, c, ada_w, ada_b, norm_mix_g, norm_ffn_g, w_in, w_out, q_norm_g, k_norm_g, cmp_pe_k, cmp_pe_v,
           cmp_w_k, cmp_w_v, gate_b, conv_w, conv_b, lru_wa, lru_ba, lru_wi, lru_bi, lru_lam, out_g_attn,
           out_g_rnn, peer_wq, peer_keys, peer_down, peer_up):
    B, S, D = x.shape
    T = B * S
    xf = x.reshape(T, D)
    mod = _mod(c, ada_w, ada_b)
    sh1, sc1, gt1, sh2, sc2, gt2 = [mod[:, k * D:(k + 1) * D] for k in range(6)]

    q0, g0 = NSA_WIDTH, NSA_WIDTH + 6 * KV_COLS
    r0 = g0 + GATE_COLS
    kvs = [w_in[:, q0 + k * KV_COLS:q0 + (k + 1) * KV_COLS] for k in range(6)]
    w_main = jnp.concatenate([w_in[:, :q0], w_in[:, r0:r0 + 2 * RNN_WIDTH]] + kvs, axis=1).astype(bf16)
    per_g = GATE_COLS // NSA_KV_HEADS
    wg = jnp.zeros((D, NSA_KV_HEADS * LANES), f32)
    gb = jnp.zeros((1, NSA_KV_HEADS * LANES), f32)
    for g in range(NSA_KV_HEADS):
        wg = wg.at[:, g * LANES:g * LANES + per_g].set(w_in[:, g0 + g * per_g:g0 + (g + 1) * per_g])
        gb = gb.at[0, g * LANES:g * LANES + per_g].set(gate_b[g * per_g:(g + 1) * per_g])
    z, gates = _norm_mm(xf, sc1, sh1, norm_mix_g, w_main, S, tm=1024, tn=768, wg=wg.astype(bf16), gb=gb)

    cos, sin = _rope_tables(S)
    q, kc_r, vc_r, ks, vs, kw, vw = _prep(z, cos, sin, q_norm_g, k_norm_g, B, S)
    kc, vc = _compress(kc_r, vc_r, cmp_pe_k, cmp_pe_v, cmp_w_k, cmp_w_v, k_norm_g[0])
    o_attn = _attention(q, kc, vc, ks, vs, kw, vw, gates)
    o_rnn = _rglru(z, conv_w, conv_b, lru_wa, lru_ba, lru_wi, lru_bi, lru_lam, B, S)
    x1 = _outproj(o_attn, o_rnn, out_g_attn, out_g_rnn, w_out.astype(bf16), xf, gt1, S)

    qp, h2 = _norm_mm(x1, sc2, sh2, norm_ffn_g, peer_wq.astype(bf16), S, tm=512, tn=2048, emit_h=True)
    eidx, gsm = _peer_topk(qp, peer_keys)
    out = _peer_ffn(eidx, gsm, h2, x1, gt2, _pack_experts(peer_down, peer_up), S)
    return out.reshape(B, S, D)


def kernel(x, c, ada_w, ada_b, norm_mix_g, norm_ffn_g, w_in, w_out, q_norm_g, k_norm_g, cmp_pe_k, cmp_pe_v, cmp_w_k, cmp_w_v, gate_b, conv_w, conv_b, lru_wa, lru_ba, lru_wi, lru_bi, lru_lam, out_g_attn, out_g_rnn, peer_wq, peer_keys, peer_down, peer_up):
    params = (ada_w, ada_b, norm_mix_g, norm_ffn_g, w_in, w_out, q_norm_g, k_norm_g, cmp_pe_k, cmp_pe_v, cmp_w_k,
              cmp_w_v, gate_b, conv_w, conv_b, lru_wa, lru_ba, lru_wi, lru_bi, lru_lam, out_g_attn, out_g_rnn,
              peer_wq, peer_keys, peer_down, peer_up)
    for l in range(ada_w.shape[0]):
        x = _layer(x, c, *[p[l] for p in params])
    return x
```

```python
import functools

import numpy as np
import jax
import jax.numpy as jnp
from jax import lax
from jax.experimental import pallas as pl
from jax.experimental.pallas import tpu as pltpu

f32 = jnp.float32
bf16 = jnp.bfloat16
i32 = jnp.int32

HEAD_DIM = 128
NSA_HEADS = 8
NSA_KV_HEADS = 2
NSA_GROUP = NSA_HEADS // NSA_KV_HEADS
NSA_WIDTH = NSA_HEADS * HEAD_DIM
KV_COLS = NSA_KV_HEADS * HEAD_DIM
GATE_COLS = 3 * NSA_HEADS
CMP_LEN = 32
CMP_STRIDE = 16
SLC_LEN = 64
N_SELECT = 16
FORCED_SCORE = 1e4
WINDOW = 512
RNN_WIDTH = 1024
RNN_BLOCKS = 8
RNN_BLOCK_DIM = RNN_WIDTH // RNN_BLOCKS
CONV_WIDTH = 4
LRU_C = 8.0
PEER_HEADS = 8
PEER_KEYS = 128
PEER_TOPK = 16
ROPE_THETA = 10000.0
EPS = 1e-6
NEG_BIG = -1e30

LANES = 128
SUBLANES = 8
VMEM_LIMIT = 56 * 1024 * 1024

NT_DIMS = (((1,), (1,)), ((), ()))


def _cparams(*sem):
    return pltpu.CompilerParams(dimension_semantics=sem, vmem_limit_bytes=VMEM_LIMIT)


def _rms(x, g):
    return x * lax.rsqrt(jnp.mean(x * x, axis=-1, keepdims=True) + EPS) * g


def _mod_kernel(c_ref, w_ref, b_ref, o_ref):
    c = c_ref[...]
    a = c * jax.nn.sigmoid(c)
    o_ref[...] = jnp.dot(a, w_ref[...], preferred_element_type=f32) + b_ref[...]


def _mod(c, w, b, tn=1024):
    B, D = c.shape
    N = w.shape[1]
    return pl.pallas_call(
        _mod_kernel,
        grid=(N // tn,),
        in_specs=[
            pl.BlockSpec((B, D), lambda j: (0, 0)),
            pl.BlockSpec((D, tn), lambda j: (0, j)),
            pl.BlockSpec((1, tn), lambda j: (0, j)),
        ],
        out_specs=pl.BlockSpec((B, tn), lambda j: (0, j)),
        out_shape=jax.ShapeDtypeStruct((B, N), f32),
        compiler_params=_cparams("arbitrary"),
        name="mod",
    )(c, w, b.reshape(1, N))


def _norm_mm_kernel(x_ref, sc_ref, sh_ref, g_ref, w_ref, *rest, n_extra, emit_h):
    rest = list(rest)
    if n_extra:
        wg_ref, gb_ref = rest.pop(0), rest.pop(0)
    z_ref = rest.pop(0)
    if n_extra:
        gate_ref = rest.pop(0)
    if emit_h:
        h_ref = rest.pop(0)
    hb_ref = rest.pop(0)

    @pl.when(pl.program_id(1) == 0)
    def _():
        h = _rms(x_ref[...], g_ref[...]) * (1.0 + sc_ref[0]) + sh_ref[0]
        hb = h.astype(bf16)
        hb_ref[...] = hb
        if emit_h:
            h_ref[...] = h
        if n_extra:
            gl = jnp.dot(hb, wg_ref[...], preferred_element_type=f32)
            gate_ref[...] = jax.nn.sigmoid(gl + gb_ref[...])

    z_ref[...] = jnp.dot(hb_ref[...], w_ref[...], preferred_element_type=f32)


def _norm_mm(x, sc, sh, g, w, seq, *, tm, tn, wg=None, gb=None, emit_h=False):
    T, D = x.shape
    N = w.shape[1]
    B = sc.shape[0]
    assert seq % tm == 0 and N % tn == 0
    per_b = seq // tm
    n_extra = 0 if wg is None else wg.shape[1]
    in_specs = [
        pl.BlockSpec((tm, D), lambda i, j: (i, 0)),
        pl.BlockSpec((1, 1, D), lambda i, j: (i // per_b, 0, 0)),
        pl.BlockSpec((1, 1, D), lambda i, j: (i // per_b, 0, 0)),
        pl.BlockSpec((1, D), lambda i, j: (0, 0)),
        pl.BlockSpec((D, tn), lambda i, j: (0, j)),
    ]
    args = [x, sc.reshape(B, 1, D), sh.reshape(B, 1, D), g.reshape(1, D), w]
    out_specs = [pl.BlockSpec((tm, tn), lambda i, j: (i, j))]
    out_shape = [jax.ShapeDtypeStruct((T, N), f32)]
    if n_extra:
        in_specs += [pl.BlockSpec((D, n_extra), lambda i, j: (0, 0)), pl.BlockSpec((1, n_extra), lambda i, j: (0, 0))]
        args += [wg, gb]
        out_specs.append(pl.BlockSpec((tm, n_extra), lambda i, j: (i, 0)))
        out_shape.append(jax.ShapeDtypeStruct((T, n_extra), f32))
    if emit_h:
        out_specs.append(pl.BlockSpec((tm, D), lambda i, j: (i, 0)))
        out_shape.append(jax.ShapeDtypeStruct((T, D), f32))
    return pl.pallas_call(
        functools.partial(_norm_mm_kernel, n_extra=n_extra, emit_h=emit_h),
        grid=(T // tm, N // tn),
        in_specs=in_specs,
        out_specs=out_specs,
        out_shape=out_shape,
        scratch_shapes=[pltpu.VMEM((tm, D), bf16)],
        compiler_params=_cparams("arbitrary", "arbitrary"),
        name="norm_mm",
    )(*args)


def _prep_kernel(zq_ref, zc_ref, zs_ref, zw_ref, cos_ref, sin_ref, qg_ref, kg_ref,
                 q_ref, kc_ref, vc_ref, ks_ref, vs_ref, kw_ref, vw_ref):
    cos = cos_ref[...]
    sin = sin_ref[...]

    def rope(y):
        return y * cos + pltpu.roll(y, HEAD_DIM // 2, 1) * sin

    def head(ref, h):
        return ref[:, h * HEAD_DIM:(h + 1) * HEAD_DIM]

    qg = qg_ref[...]
    for h in range(NSA_HEADS):
        q_ref[0, h] = rope(_rms(head(zq_ref, h), qg)).astype(bf16)
    for g in range(NSA_KV_HEADS):
        kc_ref[0, g] = rope(head(zc_ref, g))
        vc_ref[0, g] = head(zc_ref, NSA_KV_HEADS + g)
        ks_ref[0, g] = rope(_rms(head(zs_ref, g), kg_ref[1:2])).astype(bf16)
        vs_ref[0, g] = head(zs_ref, NSA_KV_HEADS + g).astype(bf16)
        kw_ref[0, g] = rope(_rms(head(zw_ref, g), kg_ref[2:3])).astype(bf16)
        vw_ref[0, g] = head(zw_ref, NSA_KV_HEADS + g).astype(bf16)


def _prep(z, cos, sin, q_g, k_g, B, S, *, ts=256):
    per_b = S // ts
    G = NSA_KV_HEADS
    kv = 2 * KV_COLS
    base = (NSA_WIDTH + 2 * RNN_WIDTH) // kv

    def zspec(width, idx):
        return pl.BlockSpec((ts, width), lambda b, i: (b * per_b + i, idx))

    def hm(nh, dt):
        return pl.BlockSpec((1, nh, ts, HEAD_DIM), lambda b, i: (b, 0, i, 0)), jax.ShapeDtypeStruct((B, nh, S, HEAD_DIM), dt)

    outs = [hm(NSA_HEADS, bf16), hm(G, f32), hm(G, f32), hm(G, bf16), hm(G, bf16), hm(G, bf16), hm(G, bf16)]
    return pl.pallas_call(
        _prep_kernel,
        grid=(B, per_b),
        in_specs=[
            zspec(NSA_WIDTH, 0), zspec(kv, base), zspec(kv, base + 1), zspec(kv, base + 2),
            pl.BlockSpec((ts, HEAD_DIM), lambda b, i: (i, 0)),
            pl.BlockSpec((ts, HEAD_DIM), lambda b, i: (i, 0)),
            pl.BlockSpec((1, HEAD_DIM), lambda b, i: (0, 0)),
            pl.BlockSpec((3, HEAD_DIM), lambda b, i: (0, 0)),
        ],
        out_specs=[o[0] for o in outs],
        out_shape=[o[1] for o in outs],
        compiler_params=_cparams("arbitrary", "arbitrary"),
        name="prep",
    )(z, z, z, z, cos, sin, q_g.reshape(1, HEAD_DIM), k_g)


def _compress_kernel(kseg_ref, vseg_ref, pek_ref, pev_ref, wk_ref, wv_ref, kg_ref, kc_ref, vc_ref, *, n_cmp):
    n_seg = kseg_ref.shape[2]
    half = CMP_STRIDE * HEAD_DIM
    row = lax.broadcasted_iota(i32, (n_seg, HEAD_DIM), 0)

    def comp(seg_ref, pe_ref, w_ref):
        seg = seg_ref[0, 0]
        lo = jnp.dot((seg + pe_ref[0:1]).astype(bf16), w_ref[0:half], preferred_element_type=f32)
        hi = jnp.dot((seg + pe_ref[1:2]).astype(bf16), w_ref[half:2 * half], preferred_element_type=f32)
        return lo + pltpu.roll(hi, n_seg - 1, 0)

    kc = _rms(comp(kseg_ref, pek_ref, wk_ref), kg_ref[...])
    vc = comp(vseg_ref, pev_ref, wv_ref)
    kc_ref[0, 0] = jnp.where(row < n_cmp, kc, 0.0).astype(bf16)
    vc_ref[0, 0] = jnp.where(row < n_cmp, vc, 0.0).astype(bf16)


def _compress(kc_r, vc_r, pe_k, pe_v, w_k, w_v, k_g0):
    B, G, S, dh = kc_r.shape
    n_seg = S // CMP_STRIDE
    n_cmp = (S - CMP_LEN) // CMP_STRIDE + 1
    half = CMP_STRIDE * dh
    seg = lambda a: a.reshape(B, G, n_seg, half)
    segspec = pl.BlockSpec((1, 1, n_seg, half), lambda b, g: (b, g, 0, 0))
    full = lambda shape: pl.BlockSpec(shape, lambda b, g: (0,) * len(shape))
    ospec = pl.BlockSpec((1, 1, n_seg, dh), lambda b, g: (b, g, 0, 0))
    return pl.pallas_call(
        functools.partial(_compress_kernel, n_cmp=n_cmp),
        grid=(B, G),
        in_specs=[segspec, segspec, full((2, half)), full((2, half)), full((2 * half, dh)), full((2 * half, dh)),
                  full((1, dh))],
        out_specs=[ospec, ospec],
        out_shape=[jax.ShapeDtypeStruct((B, G, n_seg, dh), bf16)] * 2,
        compiler_params=_cparams("arbitrary", "arbitrary"),
        name="compress",
    )(seg(kc_r), seg(vc_r), pe_k.reshape(2, half), pe_v.reshape(2, half), w_k.astype(bf16), w_v.astype(bf16),
      k_g0.reshape(1, dh))


def _attn_kernel(q_ref, kc_ref, vc_ref, ks_ref, vs_ref, kw_ref, vw_ref, gate_ref, ovt_ref, ex_ref, cb_ref, wb_ref,
                 o_ref, *, tq, tk, n_cmp, n_slc, n_sel):
    R = NSA_GROUP
    scale = HEAD_DIM ** -0.5
    exp2_scale = scale * 1.4426950408889634
    step = pl.program_id(2)
    t0 = step * tq
    q = q_ref[0].reshape(R * tq, HEAD_DIM)
    row_t4 = t0 + (lax.broadcasted_iota(i32, (R * tq, 1), 0) & (tq - 1))
    per_head = lambda a: jnp.concatenate([a] * R, axis=0)

    ncl = kc_ref.shape[2]
    cl = lax.broadcasted_iota(i32, (1, ncl), 1)
    cmp_end = jnp.where(cl < n_cmp, cl * CMP_STRIDE + (CMP_LEN - 1), jnp.int32(2 ** 30))
    s = lax.dot_general(q, kc_ref[0, 0], NT_DIMS, preferred_element_type=f32) * scale
    s = jnp.where(cmp_end <= row_t4, s, -jnp.inf)
    m = jnp.max(s, axis=-1, keepdims=True)
    m = jnp.where(m == -jnp.inf, 0.0, m)
    p = jnp.exp(s - m)
    d = jnp.sum(p, axis=-1, keepdims=True)
    p = p / jnp.where(d > 0, d, 1.0)
    o_cmp = jnp.dot(p.astype(bf16), vc_ref[0, 0], preferred_element_type=f32)

    psum = p[0:tq]
    for r in range(1, R):
        psum = psum + p[r * tq:(r + 1) * tq]
    p_hi = psum.astype(bf16)
    p_lo = (psum - p_hi.astype(f32)).astype(bf16)
    ovt = ovt_ref[...]
    imp = (lax.dot_general(ovt, p_hi, NT_DIMS, preferred_element_type=f32)
           + lax.dot_general(ovt, p_lo, NT_DIMS, preferred_element_type=f32))
    nb = ovt.shape[0]
    jb = lax.broadcasted_iota(i32, (nb, tq), 0)
    tcol = t0 + lax.broadcasted_iota(i32, (nb, tq), 1)
    cur = tcol // SLC_LEN
    imp = jnp.where(jb * SLC_LEN <= tcol, imp, -FORCED_SCORE)
    imp = jnp.where(jb == 0, FORCED_SCORE, jnp.where(jb == cur, FORCED_SCORE, jnp.where(jb == cur - 1, FORCED_SCORE, imp)))
    imp = jnp.where(jb < n_slc, imp, -jnp.inf)
    rank = jnp.zeros((nb, tq), f32)
    for i in range(n_slc):
        row = imp[i:i + 1, :]
        later = jnp.where(jb > i, 1.0, 0.0)
        rank = rank + jnp.where(row > imp, 1.0, jnp.where(row == imp, later, 0.0))
    sel_bias = jnp.where(rank < n_sel, jnp.where(jb < n_slc, 0.0, NEG_BIG), NEG_BIG)
    sel_bias = jnp.concatenate([sel_bias, jnp.full((LANES - nb, tq), NEG_BIG, f32)], axis=0).T.astype(bf16)

    def slc_chunk(c, carry, causal_bias):
        m_i, l_i, acc = carry
        k0 = pl.multiple_of(c * tk, tk)
        k = ks_ref[0, 0, pl.ds(k0, tk), :]
        v = vs_ref[0, 0, pl.ds(k0, tk), :]
        bias = jnp.dot(sel_bias, ex_ref[c], preferred_element_type=f32)
        if causal_bias is not None:
            bias = bias + causal_bias
        sc = lax.dot_general(q, k, NT_DIMS, preferred_element_type=f32) + per_head(bias)
        m_n = jnp.maximum(m_i, jnp.max(sc, axis=-1, keepdims=True))
        alpha = jnp.exp2((m_i - m_n) * exp2_scale)
        pc = jnp.exp2((sc - m_n) * exp2_scale)
        l_n = alpha * l_i + jnp.sum(pc, axis=-1, keepdims=True)
        acc_n = alpha * acc + jnp.dot(pc.astype(bf16), v, preferred_element_type=f32)
        return m_n, l_n, acc_n

    last = (t0 + tq - 1) // tk
    init = (jnp.full((R * tq, 1), NEG_BIG, f32), jnp.zeros((R * tq, 1), f32), jnp.zeros((R * tq, HEAD_DIM), f32))
    carry = lax.fori_loop(0, last, lambda c, cr: slc_chunk(c, cr, None), init)
    _, l_s, acc_s = slc_chunk(last, carry, cb_ref[(t0 - last * tk) // tq])
    o_slc = acc_s / l_s

    wk = WINDOW + tq
    w0 = pl.multiple_of(jnp.maximum(t0 - WINDOW, 0), tq)
    sw = lax.dot_general(q, kw_ref[0, 0, pl.ds(w0, wk), :], NT_DIMS, preferred_element_type=f32)
    sw = sw + per_head(wb_ref[jnp.minimum(step, WINDOW // tq)])
    pw = jnp.exp2((sw - jnp.max(sw, axis=-1, keepdims=True)) * exp2_scale)
    o_win = jnp.dot(pw.astype(bf16), vw_ref[0, 0, pl.ds(w0, wk), :], preferred_element_type=f32)
    o_win = o_win / jnp.sum(pw, axis=-1, keepdims=True)

    gates = gate_ref[...]
    for r in range(R):
        rows = slice(r * tq, (r + 1) * tq)
        o_ref[:, r * HEAD_DIM:(r + 1) * HEAD_DIM] = (
            gates[:, 3 * r:3 * r + 1] * o_cmp[rows]
            + gates[:, 3 * r + 1:3 * r + 2] * o_slc[rows]
            + gates[:, 3 * r + 2:3 * r + 3] * o_win[rows])


def _attention(q, kc, vc, ks, vs, kw, vw, gates, *, tq=128, tk=512):
    B, H, S, dh = q.shape
    G = NSA_KV_HEADS
    n_seg = kc.shape[2]
    n_cmp = (S - CMP_LEN) // CMP_STRIDE + 1
    n_slc = S // SLC_LEN
    n_sel = min(N_SELECT, n_slc)
    assert n_seg == LANES and n_slc <= LANES and S % tk == 0 and tk % tq == 0 and S >= WINDOW + tq
    per_b = S // tq
    cs = np.arange(n_seg)[:, None] * CMP_STRIDE
    ss = np.arange(LANES)[None, :] * SLC_LEN
    ov = np.maximum(np.minimum(cs + CMP_LEN, ss + SLC_LEN) - np.maximum(cs, ss), 0).astype(np.float32) / CMP_LEN
    ov[n_cmp:, :] = 0.0
    ov[:, n_slc:] = 0.0
    nb = -(-n_slc // SUBLANES) * SUBLANES
    ovt = np.ascontiguousarray(ov.T[:nb])
    kpos = np.arange(S).reshape(S // tk, 1, tk)
    ex = (kpos // SLC_LEN == np.arange(LANES)[None, :, None]).astype(np.float32)
    r_ = np.arange(tq)[None, :, None]
    cb = np.where(np.arange(tk)[None, None, :] <= np.arange(tk // tq)[:, None, None] * tq + r_, 0.0, NEG_BIG)
    wk = WINDOW + tq
    col = np.arange(wk)[None, None, :]
    early = col <= np.arange(WINDOW // tq)[:, None, None] * tq + r_
    full = (col <= r_ + WINDOW) & (col > r_)
    wb = np.where(np.concatenate([early, full], axis=0), 0.0, NEG_BIG)
    const = lambda a: pl.BlockSpec(a.shape, lambda b, g, i: (0,) * a.ndim)
    kvspec = pl.BlockSpec((1, 1, S, dh), lambda b, g, i: (b, g, 0, 0))
    cspec = pl.BlockSpec((1, 1, n_seg, dh), lambda b, g, i: (b, g, 0, 0))
    return pl.pallas_call(
        functools.partial(_attn_kernel, tq=tq, tk=tk, n_cmp=n_cmp, n_slc=n_slc, n_sel=n_sel),
        grid=(B, G, per_b),
        in_specs=[
            pl.BlockSpec((1, NSA_GROUP, tq, dh), lambda b, g, i: (b, g, i, 0)),
            cspec, cspec, kvspec, kvspec, kvspec, kvspec,
            pl.BlockSpec((tq, LANES), lambda b, g, i: (b * per_b + i, g)),
            const(ovt), const(ex), const(cb), const(wb),
        ],
        out_specs=pl.BlockSpec((tq, NSA_GROUP * dh), lambda b, g, i: (b * per_b + i, g)),
        out_shape=jax.ShapeDtypeStruct((B * S, NSA_WIDTH), f32),
        compiler_params=_cparams("arbitrary", "arbitrary", "arbitrary"),
        name="nsa_attention",
    )(q, kc, vc, ks, vs, kw, vw, gates, jnp.asarray(ovt, bf16), jnp.asarray(ex, bf16), jnp.asarray(cb, f32),
      jnp.asarray(wb, f32))


def _gelu(x):
    return 0.5 * x * (1.0 + lax.erf(x * (2.0 ** -0.5)))


def _rglru_kernel(xr_ref, xg_ref, cw_ref, cb_ref, wa_ref, ba_ref, wi_ref, bi_ref, lam_ref, o_ref,
                  ext_ref, a_ref, b_ref, hs_ref, hc_ref, *, ts):
    halo = 8

    @pl.when(pl.program_id(1) == 0)
    def _():
        ext_ref[0:halo] = jnp.zeros((halo, RNN_WIDTH), f32)
        hc_ref[...] = jnp.zeros_like(hc_ref)

    xr = xr_ref[...]
    ext_ref[halo:halo + ts] = xr
    u = cb_ref[...] + cw_ref[CONV_WIDTH - 1:CONV_WIDTH] * xr
    for k in range(1, CONV_WIDTH):
        u = u + cw_ref[CONV_WIDTH - 1 - k:CONV_WIDTH - k] * ext_ref[halo - k:halo - k + ts]
    ext_ref[0:halo] = xr[ts - halo:ts]

    ra, ri = [], []
    for n in range(RNN_BLOCKS):
        ub = u[:, n * RNN_BLOCK_DIM:(n + 1) * RNN_BLOCK_DIM].astype(bf16)
        ra.append(jnp.dot(ub, wa_ref[n], preferred_element_type=f32))
        ri.append(jnp.dot(ub, wi_ref[n], preferred_element_type=f32))
    r = jax.nn.sigmoid(jnp.concatenate(ra, axis=1) + ba_ref[...])
    ig = jax.nn.sigmoid(jnp.concatenate(ri, axis=1) + bi_ref[...])
    log_a = (-LRU_C) * r * jax.nn.softplus(-lam_ref[...])
    a_ref[...] = jnp.exp(log_a)
    b_ref[...] = jnp.sqrt(1.0 - jnp.exp(2.0 * log_a)) * ig * u

    sub = lax.broadcasted_iota(i32, (8, RNN_WIDTH), 0)

    def blk(j, h):
        r0 = pl.multiple_of(j * 8, 8)
        a8 = a_ref[pl.ds(r0, 8), :]
        b8 = b_ref[pl.ds(r0, 8), :]
        out = jnp.zeros((8, RNN_WIDTH), f32)
        for i in range(8):
            h = a8[i:i + 1] * h + b8[i:i + 1]
            out = jnp.where(sub == i, h, out)
        hs_ref[pl.ds(r0, 8), :] = out
        return h

    hc_ref[...] = lax.fori_loop(0, ts // 8, blk, hc_ref[...])
    o_ref[...] = _gelu(xg_ref[...]) * hs_ref[...]


def _rglru(z, conv_w, conv_b, wa, ba, wi, bi, lam, B, S, *, ts=512):
    W = RNN_WIDTH
    per_b = S // ts
    xr_idx = NSA_WIDTH // W
    vec = lambda a: a.reshape(1, W)
    full = lambda shape: pl.BlockSpec(shape, lambda b, i: (0,) * len(shape))
    return pl.pallas_call(
        functools.partial(_rglru_kernel, ts=ts),
        grid=(B, per_b),
        in_specs=[
            pl.BlockSpec((ts, W), lambda b, i: (b * per_b + i, xr_idx)),
            pl.BlockSpec((ts, W), lambda b, i: (b * per_b + i, xr_idx + 1)),
            full((CONV_WIDTH, W)), full((1, W)),
            full((RNN_BLOCKS, RNN_BLOCK_DIM, RNN_BLOCK_DIM)), full((1, W)),
            full((RNN_BLOCKS, RNN_BLOCK_DIM, RNN_BLOCK_DIM)), full((1, W)), full((1, W)),
        ],
        out_specs=pl.BlockSpec((ts, W), lambda b, i: (b * per_b + i, 0)),
        out_shape=jax.ShapeDtypeStruct((B * S, W), f32),
        scratch_shapes=[pltpu.VMEM((ts + 8, W), f32), pltpu.VMEM((ts, W), f32), pltpu.VMEM((ts, W), f32),
                        pltpu.VMEM((ts, W), f32), pltpu.VMEM((1, W), f32)],
        compiler_params=_cparams("arbitrary", "arbitrary"),
        name="rglru",
    )(z, z, conv_w, vec(conv_b), wa.astype(bf16), vec(ba), wi.astype(bf16), vec(bi), vec(lam))


def _outproj_kernel(oa_ref, or_ref, ga_ref, gr_ref, w_ref, x_ref, gt_ref, o_ref, yb_ref):
    @pl.when(pl.program_id(1) == 0)
    def _():
        wa = oa_ref.shape[1]
        yb_ref[:, 0:wa] = _rms(oa_ref[...], ga_ref[...]).astype(bf16)
        yb_ref[:, wa:] = _rms(or_ref[...], gr_ref[...]).astype(bf16)

    o_ref[...] = x_ref[...] + gt_ref[0] * jnp.dot(yb_ref[...], w_ref[...], preferred_element_type=f32)


def _outproj(o_attn, o_rnn, g_a, g_r, w, x, gt, seq, *, tm=512, tn=2048):
    T, D = x.shape
    B = gt.shape[0]
    wa, wr = o_attn.shape[1], o_rnn.shape[1]
    per_b = seq // tm
    return pl.pallas_call(
        _outproj_kernel,
        grid=(T // tm, D // tn),
        in_specs=[
            pl.BlockSpec((tm, wa), lambda i, j: (i, 0)),
            pl.BlockSpec((tm, wr), lambda i, j: (i, 0)),
            pl.BlockSpec((1, wa), lambda i, j: (0, 0)),
            pl.BlockSpec((1, wr), lambda i, j: (0, 0)),
            pl.BlockSpec((wa + wr, tn), lambda i, j: (0, j)),
            pl.BlockSpec((tm, tn), lambda i, j: (i, j)),
            pl.BlockSpec((1, 1, tn), lambda i, j: (i // per_b, 0, j)),
        ],
        out_specs=pl.BlockSpec((tm, tn), lambda i, j: (i, j)),
        out_shape=jax.ShapeDtypeStruct((T, D), f32),
        scratch_shapes=[pltpu.VMEM((tm, wa + wr), bf16)],
        compiler_params=_cparams("arbitrary", "arbitrary"),
        name="out_proj",
    )(o_attn, o_rnn, g_a.reshape(1, wa), g_r.reshape(1, wr), w, x, gt.reshape(B, 1, D))


def _top_rows(s, k):
    n = s.shape[0]
    io = lax.broadcasted_iota(i32, s.shape, 0)
    vals, idxs = [], []
    for _ in range(k):
        m = jnp.max(s, axis=0, keepdims=True)
        ix = jnp.min(jnp.where(s == m, io, n), axis=0, keepdims=True)
        vals.append(m)
        idxs.append(ix)
        s = jnp.where(io == ix, -jnp.inf, s)
    return vals, idxs


def _peer_topk_kernel(q_ref, keys_ref, e_ref, g_ref):
    K = PEER_TOPK
    tt = q_ref.shape[0]
    pairs = [(a, b) for a in range(K) for b in range(K) if (a + 1) * (b + 1) <= K]
    n_cand = len(pairs) + (-len(pairs) % 8)
    cio = lax.broadcasted_iota(i32, (n_cand, tt), 0)
    oio = lax.broadcasted_iota(i32, (PEER_HEADS * K, tt), 0)
    e_all = jnp.zeros((PEER_HEADS * K, tt), i32)
    g_all = jnp.zeros((PEER_HEADS * K, tt), f32)
    for h in range(PEER_HEADS):
        tops = []
        for p in range(2):
            c0 = (2 * h + p) * PEER_KEYS
            qhp = q_ref[:, c0:c0 + PEER_KEYS].astype(bf16)
            s = lax.dot_general(keys_ref[h, p], qhp, NT_DIMS, preferred_element_type=f32)
            tops.append(_top_rows(s, K))
        (v1, i1), (v2, i2) = tops
        cand = jnp.full((n_cand, tt), -jnp.inf, f32)
        cidx = jnp.zeros((n_cand, tt), i32)
        for r, (a, b) in enumerate(pairs):
            cand = jnp.where(cio == r, v1[a] + v2[b], cand)
            cidx = jnp.where(cio == r, i1[a] * PEER_KEYS + i2[b], cidx)
        vals, pos = _top_rows(cand, K)
        ex = [jnp.exp(v - vals[0]) for v in vals]
        den = functools.reduce(lambda x, y: x + y, ex)
        for r in range(K):
            e_r = jnp.sum(jnp.where(cio == pos[r], cidx, 0), axis=0, keepdims=True)
            e_all = jnp.where(oio == h * K + r, e_r, e_all)
            g_all = jnp.where(oio == h * K + r, ex[r] / den, g_all)
    e_ref[...] = e_all.T
    g_ref[...] = g_all.T


def _peer_topk(qp, keys, *, tt=256):
    T, N = qp.shape
    HK = PEER_HEADS * PEER_TOPK
    return pl.pallas_call(
        _peer_topk_kernel,
        grid=(T // tt,),
        in_specs=[pl.BlockSpec((tt, N), lambda i: (i, 0)),
                  pl.BlockSpec(keys.shape, lambda i: (0, 0, 0, 0))],
        out_specs=[pl.BlockSpec((tt, HK), lambda i: (i, 0))] * 2,
        out_shape=[jax.ShapeDtypeStruct((T, HK), i32), jax.ShapeDtypeStruct((T, HK), f32)],
        compiler_params=_cparams("arbitrary"),
        name="peer_topk",
    )(qp, keys.astype(bf16))


PEER_GROUP = 2
PEER_AHEAD = 3
PEER_SLOTS = PEER_GROUP * (PEER_AHEAD + 1)


def _sublane_sums(qs, sub):
    order = [0, 4, 2, 6, 1, 5, 3, 7]
    level = [qs[i] for i in order]
    for dist in (4, 2, 1):
        first = (sub & dist) == 0
        nxt = []
        for a, b in zip(level[0::2], level[1::2]):
            nxt.append(jnp.where(first, a, pltpu.roll(b, dist, 0)) + jnp.where(first, pltpu.roll(a, SUBLANES - dist, 0), b))
        level = nxt
    return level[0]


def _peer_ffn_kernel(idx_ref, g_ref, h_ref, x_ref, gt_ref, tab_ref, o_ref, *scratch, tt):
    bufs, sem_ref = scratch[:PEER_SLOTS], scratch[PEER_SLOTS]
    NE = idx_ref.shape[1]
    n_lt = h_ref.shape[1] // LANES
    n_grp = NE // SUBLANES
    per_step = NE // (2 * n_grp)
    ahead_tokens = PEER_GROUP * PEER_AHEAD
    step = pl.program_id(0)
    assert n_lt == 2 * SUBLANES

    def issue(t, slot, ks):
        for k in ks:
            pltpu.make_async_copy(tab_ref.at[idx_ref[t, k]], bufs[slot].at[k], sem_ref.at[slot]).start(priority=k % 2)

    def wait(slot):
        pltpu.make_async_copy(bufs[slot], bufs[slot], sem_ref.at[slot]).wait()

    diag = lax.broadcasted_iota(i32, (NE, LANES), 0) == lax.broadcasted_iota(i32, (NE, LANES), 1)
    sub = lax.broadcasted_iota(i32, (SUBLANES, LANES), 0)

    def evaluate(t, slot, ahead):
        nslot = (slot + ahead_tokens) % PEER_SLOTS
        buf = bufs[slot]

        def step_done(i):
            if ahead:
                issue(t + ahead_tokens, nslot, range(i * per_step, (i + 1) * per_step))

        xrow = h_ref[pl.ds(t, 1), :]
        xa = jnp.zeros((SUBLANES, LANES), f32)
        xb = jnp.zeros((SUBLANES, LANES), f32)
        for c in range(SUBLANES):
            xa = jnp.where(sub == c, xrow[:, c * LANES:(c + 1) * LANES], xa)
            xb = jnp.where(sub == c, xrow[:, (SUBLANES + c) * LANES:(SUBLANES + c + 1) * LANES], xb)
        parts = []
        for g in range(n_grp):
            qs = []
            for j in range(SUBLANES):
                k = g * SUBLANES + j
                dk = buf[k, 0:n_lt, :].astype(f32)
                qs.append(dk[0:SUBLANES] * xa + dk[SUBLANES:] * xb)
            parts.append(_sublane_sums(qs, sub))
            step_done(g)
        part = jnp.concatenate(parts, axis=0)
        g_col = jnp.sum(jnp.where(diag, g_ref[pl.ds(t, 1), :], 0.0), axis=1, keepdims=True)
        act = jnp.sum(part, axis=1, keepdims=True)
        w_col = jnp.broadcast_to(_gelu(act) * g_col, (NE, LANES))
        acc_a = [jnp.zeros((SUBLANES, LANES), f32) for _ in range(4)]
        acc_b = [jnp.zeros((SUBLANES, LANES), f32) for _ in range(4)]
        for g in range(n_grp):
            for j in range(SUBLANES):
                k = g * SUBLANES + j
                w = jnp.broadcast_to(w_col[k:k + 1, :], (SUBLANES, LANES))
                uk = buf[k, n_lt:, :].astype(f32)
                acc_a[j % 4] = acc_a[j % 4] + uk[0:SUBLANES] * w
                acc_b[j % 4] = acc_b[j % 4] + uk[SUBLANES:] * w
            step_done(n_grp + g)
        ya = (acc_a[0] + acc_a[1]) + (acc_a[2] + acc_a[3])
        yb = (acc_b[0] + acc_b[1]) + (acc_b[2] + acc_b[3])
        y = jnp.concatenate([half[c:c + 1, :] for half in (ya, yb) for c in range(SUBLANES)], axis=1)
        o_ref[pl.ds(t, 1), :] = x_ref[pl.ds(t, 1), :] + gt_ref[0] * y

    def group(t0, s0, ahead):
        for u in range(PEER_GROUP):
            wait(s0 + u)
        for u in range(PEER_GROUP):
            evaluate(t0 + u, s0 + u, ahead)

    def prime(_, carry):
        for t in range(ahead_tokens):
            issue(t, t, range(NE))
        return carry

    lax.fori_loop(0, jnp.where(step == 0, 1, 0), prime, 0)

    def steady(j, carry):
        for s0 in range(0, PEER_SLOTS, PEER_GROUP):
            group(j * PEER_SLOTS + s0, s0, True)
        return carry

    lax.fori_loop(0, tt // PEER_SLOTS, steady, 0)

    @pl.when(step == pl.num_programs(0) - 1)
    def _():
        for t in range(ahead_tokens):
            wait(t)


def _peer_ffn(eidx, gsm, h2, x1, gt2, table, seq, *, tt=128):
    T, D = h2.shape
    B = gt2.shape[0]
    NE = eidx.shape[1]
    ahead_tokens = PEER_GROUP * PEER_AHEAD
    n_lt = D // LANES
    assert tt % PEER_SLOTS == 0 and tt >= 2 * PEER_SLOTS and NE % SUBLANES == 0 and ahead_tokens <= SUBLANES
    per_b = seq // tt
    eidx_ext = jnp.concatenate([eidx, eidx[T - SUBLANES:]], axis=0)
    tokens = pl.BlockSpec((tt, D), lambda i: (i, 0))
    return pl.pallas_call(
        functools.partial(_peer_ffn_kernel, tt=tt),
        grid=(T // tt,),
        in_specs=[
            pl.BlockSpec((pl.Element(tt + SUBLANES), pl.Element(NE)), lambda i: (i * tt, 0), memory_space=pltpu.SMEM),
            pl.BlockSpec((tt, NE), lambda i: (i, 0)),
            tokens,
            tokens,
            pl.BlockSpec((1, 1, D), lambda i: (i // per_b, 0, 0)),
            pl.BlockSpec(memory_space=pl.ANY),
        ],
        out_specs=tokens,
        out_shape=jax.ShapeDtypeStruct((T, D), f32),
        scratch_shapes=([pltpu.VMEM((NE, 2 * n_lt, LANES), bf16)] * PEER_SLOTS
                        + [pltpu.SemaphoreType.DMA((PEER_SLOTS,))]),
        compiler_params=_cparams("arbitrary"),
        name="peer_ffn",
    )(eidx_ext, gsm, h2, x1, gt2.reshape(B, 1, D), table)


def _pack_kernel(d_ref, u_ref, o_ref, rec_ref):
    n_lt = d_ref.shape[1] // LANES
    for c in range(n_lt):
        rec_ref[:, c, :] = d_ref[:, c * LANES:(c + 1) * LANES]
        rec_ref[:, n_lt + c, :] = u_ref[:, c * LANES:(c + 1) * LANES]
    o_ref[...] = rec_ref[...].astype(bf16)


def _pack_experts(down, up, *, tb=256):
    E, D = down.shape
    spec = pl.BlockSpec((tb, D), lambda i: (i, 0))
    return pl.pallas_call(
        _pack_kernel,
        grid=(E // tb,),
        in_specs=[spec, spec],
        out_specs=pl.BlockSpec((tb, 2 * D // LANES, LANES), lambda i: (i, 0, 0)),
        out_shape=jax.ShapeDtypeStruct((E, 2 * D // LANES, LANES), bf16),
        scratch_shapes=[pltpu.VMEM((tb, 2 * D // LANES, LANES), f32)],
        compiler_params=_cparams("arbitrary"),
        name="pack_experts",
    )(down, up)


def _rope_tables(S):
    half = HEAD_DIM // 2
    freqs = ROPE_THETA ** (-jnp.arange(half, dtype=f32) / half)
    ang = jnp.arange(S, dtype=f32)[:, None] * freqs[None, :]
    cos, sin = jnp.cos(ang), jnp.sin(ang)
    return jnp.concatenate([cos, cos], axis=1), jnp.concatenate([-sin, sin], axis=1)


def _layer(x, c, ada_w, ada_b, norm_mix_g, norm_ffn_g, w_in, w_out, q_norm_g, k_norm_g, cmp_pe_k, cmp_pe_v,
           cmp_w_k, cmp_w_v, gate_b, conv_w, conv_b, lru_wa, lru_ba, lru_wi, lru_bi, lru_lam, out_g_attn,
           out_g_rnn, peer_wq, peer_keys, peer_down, peer_up):
    B, S, D = x.shape
    T = B * S
    xf = x.reshape(T, D)
    mod = _mod(c, ada_w, ada_b)
    sh1, sc1, gt1, sh2, sc2, gt2 = [mod[:, k * D:(k + 1) * D] for k in range(6)]

    q0, g0 = NSA_WIDTH, NSA_WIDTH + 6 * KV_COLS
    r0 = g0 + GATE_COLS
    kvs = [w_in[:, q0 + k * KV_COLS:q0 + (k + 1) * KV_COLS] for k in range(6)]
    w_main = jnp.concatenate([w_in[:, :q0], w_in[:, r0:r0 + 2 * RNN_WIDTH]] + kvs, axis=1).astype(bf16)
    per_g = GATE_COLS // NSA_KV_HEADS
    wg = jnp.zeros((D, NSA_KV_HEADS * LANES), f32)
    gb = jnp.zeros((1, NSA_KV_HEADS * LANES), f32)
    for g in range(NSA_KV_HEADS):
        wg = wg.at[:, g * LANES:g * LANES + per_g].set(w_in[:, g0 + g * per_g:g0 + (g + 1) * per_g])
        gb = gb.at[0, g * LANES:g * LANES + per_g].set(gate_b[g * per_g:(g + 1) * per_g])
    z, gates = _norm_mm(xf, sc1, sh1, norm_mix_g, w_main, S, tm=1024, tn=768, wg=wg.astype(bf16), gb=gb)

    cos, sin = _rope_tables(S)
    q, kc_r, vc_r, ks, vs, kw, vw = _prep(z, cos, sin, q_norm_g, k_norm_g, B, S)
    kc, vc = _compress(kc_r, vc_r, cmp_pe_k, cmp_pe_v, cmp_w_k, cmp_w_v, k_norm_g[0])
    o_attn = _attention(q, kc, vc, ks, vs, kw, vw, gates)
    o_rnn = _rglru(z, conv_w, conv_b, lru_wa, lru_ba, lru_wi, lru_bi, lru_lam, B, S)
    x1 = _outproj(o_attn, o_rnn, out_g_attn, out_g_rnn, w_out.astype(bf16), xf, gt1, S)

    qp, h2 = _norm_mm(x1, sc2, sh2, norm_ffn_g, peer_wq.astype(bf16), S, tm=512, tn=2048, emit_h=True)
    eidx, gsm = _peer_topk(qp, peer_keys)
    out = _peer_ffn(eidx, gsm, h2, x1, gt2, _pack_experts(peer_down, peer_up), S)
    return out.reshape(B, S, D)


def kernel(x, c, ada_w, ada_b, norm_mix_g, norm_ffn_g, w_in, w_out, q_norm_g, k_norm_g, cmp_pe_k, cmp_pe_v, cmp_w_k, cmp_w_v, gate_b, conv_w, conv_b, lru_wa, lru_ba, lru_wi, lru_bi, lru_lam, out_g_attn, out_g_rnn, peer_wq, peer_keys, peer_down, peer_up):
    params = (ada_w, ada_b, norm_mix_g, norm_ffn_g, w_in, w_out, q_norm_g, k_norm_g, cmp_pe_k, cmp_pe_v, cmp_w_k,
              cmp_w_v, gate_b, conv_w, conv_b, lru_wa, lru_ba, lru_wi, lru_bi, lru_lam, out_g_attn, out_g_rnn,
              peer_wq, peer_keys, peer_down, peer_up)
    for l in range(ada_w.shape[0]):
        x = _layer(x, c, *[p[l] for p in params])
    return x
```
